```python
import jax, jax.numpy as jnp
from jax import lax
import numpy as np

D_MODEL = 1024
BATCH = 16
SEQ = 256
DEPTH = 4
DEC_BATCH = 2
DEC_SEQ = 4096
PAST_LEN = 256

GRID_W = 64
N_MIXERS = 2
N_POOL_LAYERS = (DEPTH + 1) // 2
N_RET_LAYERS = DEPTH // 2
POOL_WINDOWS = (2, 4, 8, 16)
POOL_GROUPS = 4
POOL_GC = D_MODEL // POOL_GROUPS
RET_HEADS = 4
RET_DK = D_MODEL // RET_HEADS
RET_DV = 2 * RET_DK
RET_HK = RET_HEADS * RET_DK
RET_HV = RET_HEADS * RET_DV
RET_IN = 2 * RET_HK + 2 * RET_HV
RET_CHUNK = 128
D_FF = 4 * D_MODEL
ROPE_BASE = 10000.0
NORM_EPS = 1e-6
GN_EPS = 1e-5
N_MOD = 6

kernel_name = "hybrid_pool_retention_diffusion_step"


def rmsnorm(x, w):
    xf = x.astype(jnp.float32)
    y = xf * lax.rsqrt(jnp.mean(jnp.square(xf), axis=-1, keepdims=True) + NORM_EPS)
    return (y * w.astype(jnp.float32)).astype(x.dtype)


def pool_mean_1d(x, w, axis):
    L = x.shape[axis]
    cs = jnp.cumsum(x.astype(jnp.float32), axis=axis)
    cs = jnp.pad(cs, [(1, 0) if a == axis else (0, 0) for a in range(x.ndim)])
    t = jnp.arange(L)
    lo = jnp.clip(t - w // 2, 0, L)
    hi = jnp.clip(t - w // 2 + w, 0, L)
    s = jnp.take(cs, hi, axis=axis) - jnp.take(cs, lo, axis=axis)
    cnt_shape = [L if a == axis else 1 for a in range(x.ndim)]
    cnt = (hi - lo).astype(jnp.float32).reshape(cnt_shape)
    return s / cnt


def pool_mixer(h, pw, pb, ps, grid):
    B, L, _ = h.shape
    outs = []
    for g, w in enumerate(POOL_WINDOWS):
        xg = h[..., g * POOL_GC:(g + 1) * POOL_GC]
        if grid:
            rows = L // GRID_W
            xr = xg.reshape(B, rows, GRID_W, POOL_GC)
            m = pool_mean_1d(pool_mean_1d(xr, w, 1), w, 2).reshape(B, L, POOL_GC)
        else:
            m = pool_mean_1d(xg, w, 1)
        d = (m - xg.astype(jnp.float32)).astype(h.dtype)
        outs.append(d @ pw[g] + pb[g])
    return jnp.concatenate(outs, axis=-1) * ps


def rope_2d(x):
    L = x.shape[2]
    t = jnp.arange(L)
    row = (t // GRID_W).astype(jnp.float32)
    col = (t % GRID_W).astype(jnp.float32)
    half = RET_DK // 2
    quarter = half // 2
    freqs = ROPE_BASE ** (-jnp.arange(quarter, dtype=jnp.float32) / quarter)

    def rot(xa, pos):
        ang = pos[:, None] * freqs[None, :]
        cos = jnp.cos(ang).astype(x.dtype)
        sin = jnp.sin(ang).astype(x.dtype)
        x1, x2 = xa[..., :quarter], xa[..., quarter:]
        return jnp.concatenate([x1 * cos - x2 * sin, x2 * cos + x1 * sin], axis=-1)

    return jnp.concatenate([rot(x[..., :half], row), rot(x[..., half:], col)], axis=-1)


def retention_chunked(q, k, v, log_gamma, s0, strict):
    b, h, L, _ = q.shape
    dv = v.shape[-1]
    n = L // RET_CHUNK

    def to_chunks(a):
        return jnp.moveaxis(a.astype(jnp.float32).reshape(b, h, n, RET_CHUNK, a.shape[-1]), 2, 0)

    qc, kc, vc = to_chunks(q), to_chunks(k), to_chunks(v)
    idx = jnp.arange(RET_CHUNK, dtype=jnp.float32)
    diff = idx[:, None] - idx[None, :]
    mask = (diff > 0) if strict else (diff >= 0)
    lg = log_gamma.astype(jnp.float32)
    dmat = jnp.where(mask[None], jnp.exp(lg[:, None, None] * jnp.maximum(diff, 0.0)[None]), 0.0)
    xi = jnp.exp(lg[:, None] * (idx + 1.0)[None])[:, :, None]
    zeta = jnp.exp(lg[:, None] * (RET_CHUNK - 1.0 - idx)[None])[:, :, None]
    g_chunk = jnp.exp(lg * RET_CHUNK)[:, None, None]

    def step(S, inp):
        qb, kb, vb = inp
        scores = jnp.einsum('bhid,bhjd->bhij', qb, kb) * dmat
        o = jnp.einsum('bhij,bhje->bhie', scores, vb) + jnp.einsum('bhid,bhde->bhie', qb * xi, S)
        S = g_chunk * S + jnp.einsum('bhjd,bhje->bhde', kb * zeta, vb)
        return S, o

    s_fin, oc = lax.scan(step, s0.astype(jnp.float32), (qc, kc, vc))
    o = jnp.moveaxis(oc, 0, 2).reshape(b, h, L, dv)
    return o, s_fin


def retention_mixer(h, w_in, decay_exp, gn_w, w_out, s0_f, s0_b, grid):
    B, L, _ = h.shape
    proj = h @ w_in
    q, k, v, g = jnp.split(proj, [RET_HK, 2 * RET_HK, 2 * RET_HK + RET_HV], axis=-1)

    def heads(a, d):
        return a.reshape(B, L, RET_HEADS, d).transpose(0, 2, 1, 3)

    q = heads(q, RET_DK)
    k = heads(k, RET_DK) * (RET_DK ** -0.5)
    v = heads(v, RET_DV)
    if grid:
        q = rope_2d(q)
        k = rope_2d(k)
    lg = jnp.log1p(-jnp.exp2(-decay_exp.astype(jnp.float32)))
    o_f, s_f = retention_chunked(q, k, v, lg[0], s0_f, False)
    o_b, s_b = retention_chunked(q[:, :, ::-1], k[:, :, ::-1], v[:, :, ::-1], lg[1], s0_b, True)
    o = o_f + o_b[:, :, ::-1]
    mu = jnp.mean(o, axis=-1, keepdims=True)
    var = jnp.mean(jnp.square(o - mu), axis=-1, keepdims=True)
    o = (o - mu) * lax.rsqrt(var + GN_EPS) * gn_w.astype(jnp.float32)[:, None, :]
    o = o.transpose(0, 2, 1, 3).reshape(B, L, RET_HV).astype(h.dtype)
    out = (jax.nn.silu(g) * o) @ w_out
    return out, s_f, s_b


def sq_relu_mlp(h, w1, w2):
    return jnp.square(jax.nn.relu(h @ w1)) @ w2


def trunk(x, cond, state_ret, grid, w_ada, b_ada, norm_mix_w, norm_mlp_w, pool_w, pool_b,
          pool_scale, ret_w_in, ret_decay, ret_gn_w, ret_w_out, mlp_w1, mlp_w2, final_norm_w):
    B = x.shape[0]
    new_states = []
    for i in range(DEPTH):
        j = i // N_MIXERS
        mod = (jax.nn.silu(cond) @ w_ada[i] + b_ada[i])[:, None, :]
        sh_a, sc_a, g_a, sh_m, sc_m, g_m = jnp.split(mod, N_MOD, axis=-1)
        h = rmsnorm(x, norm_mix_w[i]) * (1 + sc_a) + sh_a
        if i % N_MIXERS == 0:
            mix = pool_mixer(h, pool_w[j], pool_b[j], pool_scale[j], grid)
        else:
            if state_ret is None:
                s0 = jnp.zeros((B, RET_HEADS, RET_DK, RET_DV), jnp.float32)
                s0_f, s0_b = s0, s0
            else:
                s0_f, s0_b = state_ret[:, j, 0], state_ret[:, j, 1]
            mix, s_f, s_b = retention_mixer(h, ret_w_in[j], ret_decay[j], ret_gn_w[j], ret_w_out[j],
                                            s0_f, s0_b, grid)
            if state_ret is None:
                new_states.append(jnp.stack([s_f, s_b], axis=1))
        x = x + g_a * mix
        h = rmsnorm(x, norm_mlp_w[i]) * (1 + sc_m) + sh_m
        x = x + g_m * sq_relu_mlp(h, mlp_w1[i], mlp_w2[i])
    y = rmsnorm(x, final_norm_w)
    if state_ret is None:
        return y, jnp.stack(new_states, axis=1)
    return y, None


def setup_inputs(seed: int = 0) -> dict:
    key = jax.random.key(seed)
    ks = jax.random.split(key, 20)
    f32 = jnp.float32
    nrm = lambda k, s: jax.random.normal(k, s, f32)
    return {
        "x_prompt": nrm(ks[0], (BATCH, SEQ, D_MODEL)),
        "x_sample": nrm(ks[1], (DEC_BATCH, DEC_SEQ, D_MODEL)),
        "state_ret": 0.5 * nrm(ks[2], (DEC_BATCH, N_RET_LAYERS, 2, RET_HEADS, RET_DK, RET_DV)),
        "c": nrm(ks[3], (DEC_BATCH, D_MODEL)),
        "c_ctx": nrm(ks[4], (D_MODEL,)),
        "w_ada": 0.5 * D_MODEL ** -0.5 * nrm(ks[5], (DEPTH, D_MODEL, N_MOD * D_MODEL)),
        "b_ada": 0.01 * nrm(ks[6], (DEPTH, N_MOD * D_MODEL)),
        "norm_mix_w": 1.0 + 0.05 * nrm(ks[7], (DEPTH, D_MODEL)),
        "norm_mlp_w": 1.0 + 0.05 * nrm(ks[8], (DEPTH, D_MODEL)),
        "pool_w": POOL_GC ** -0.5 * nrm(ks[9], (N_POOL_LAYERS, POOL_GROUPS, POOL_GC, POOL_GC)),
        "pool_b": 0.01 * nrm(ks[10], (N_POOL_LAYERS, POOL_GROUPS, POOL_GC)),
        "pool_scale": 1.0 + 0.1 * nrm(ks[11], (N_POOL_LAYERS, D_MODEL)),
        "ret_w_in": D_MODEL ** -0.5 * nrm(ks[12], (N_RET_LAYERS, D_MODEL, RET_IN)),
        "ret_decay": 5.0 + jnp.arange(RET_HEADS, dtype=f32)[None, None, :]
                     + 0.1 * nrm(ks[13], (N_RET_LAYERS, 2, RET_HEADS)),
        "ret_gn_w": 1.0 + 0.05 * nrm(ks[14], (N_RET_LAYERS, RET_HEADS, RET_DV)),
        "ret_w_out": RET_HV ** -0.5 * nrm(ks[15], (N_RET_LAYERS, RET_HV, D_MODEL)),
        "mlp_w1": D_MODEL ** -0.5 * nrm(ks[16], (DEPTH, D_MODEL, D_FF)),
        "mlp_w2": D_FF ** -0.5 * nrm(ks[17], (DEPTH, D_FF, D_MODEL)),
        "final_norm_w": 1.0 + 0.05 * nrm(ks[18], (D_MODEL,)),
    }


def reference(x_prompt, x_sample, state_ret, c, c_ctx, w_ada, b_ada, norm_mix_w, norm_mlp_w,
              pool_w, pool_b, pool_scale, ret_w_in, ret_decay, ret_gn_w, ret_w_out, mlp_w1,
              mlp_w2, final_norm_w):
    y_prompt, new_state_ret = trunk(x_prompt, c_ctx[None, :], None, False, w_ada, b_ada,
                                    norm_mix_w, norm_mlp_w, pool_w, pool_b, pool_scale, ret_w_in,
                                    ret_decay, ret_gn_w, ret_w_out, mlp_w1, mlp_w2, final_norm_w)
    y_sample, _ = trunk(x_sample, c, state_ret, True, w_ada, b_ada, norm_mix_w, norm_mlp_w,
                        pool_w, pool_b, pool_scale, ret_w_in, ret_decay, ret_gn_w, ret_w_out,
                        mlp_w1, mlp_w2, final_norm_w)
    return (y_prompt, y_sample, new_state_ret)
```

```python
import functools

import jax
import jax.numpy as jnp
from jax import lax
from jax.experimental import pallas as pl
from jax.experimental.pallas import tpu as pltpu

F32 = jnp.float32
BF16 = jnp.bfloat16

D_MODEL = 1024
DEPTH = 4
GRID_W = 64
POOL_WINDOWS = (2, 4, 8, 16)
POOL_GC = 256
RET_HEADS = 4
RET_DK = 256
RET_DV = 512
RET_HK = RET_HEADS * RET_DK
RET_HV = RET_HEADS * RET_DV
RET_IN = 2 * RET_HK + 2 * RET_HV
D_FF = 4 * D_MODEL
ROPE_BASE = 10000.0
NORM_EPS = 1e-6
GN_EPS = 1e-5
N_MOD = 6

LANES = 128
SLAB = 256
VMEM_LIMIT = 56 * 1024 * 1024


def _cparams(sem):
    return pltpu.CompilerParams(dimension_semantics=sem, vmem_limit_bytes=VMEM_LIMIT)


def _norm_mod(x, nw, sc, sh):
    ms = jnp.mean(x * x, axis=-1, keepdims=True)
    return (x * lax.rsqrt(ms + NORM_EPS)) * nw * (1.0 + sc) + sh


def _mods_kernel(c_ref, w_ref, b_ref, o_ref):
    c = c_ref[...]
    s = c * jax.nn.sigmoid(c)
    w = w_ref[0].astype(BF16)
    o_ref[0] = jnp.dot(s.astype(BF16), w, preferred_element_type=F32) + b_ref[0]


def _mods(cond8, w_ada, b_ada):
    tn = 1024
    n = N_MOD * D_MODEL
    return pl.pallas_call(
        _mods_kernel,
        out_shape=jax.ShapeDtypeStruct((DEPTH, 8, n), F32),
        grid=(DEPTH, n // tn),
        in_specs=[
            pl.BlockSpec((8, D_MODEL), lambda l, j: (0, 0)),
            pl.BlockSpec((1, D_MODEL, tn), lambda l, j: (l, 0, j)),
            pl.BlockSpec((1, 1, tn), lambda l, j: (l, 0, j)),
        ],
        out_specs=pl.BlockSpec((1, 8, tn), lambda l, j: (l, 0, j)),
        compiler_params=_cparams(("parallel", "parallel")),
        name="mods",
    )(cond8, w_ada, b_ada.reshape(DEPTH, 1, n))


def _rms_kernel(x_ref, o_ref):
    x = x_ref[...]
    ms = jnp.mean(x * x, axis=-1, keepdims=True)
    o_ref[...] = jnp.broadcast_to(lax.rsqrt(ms + NORM_EPS), o_ref.shape)


def _rms(x2d, tm):
    t = x2d.shape[0]
    return pl.pallas_call(
        _rms_kernel,
        out_shape=jax.ShapeDtypeStruct((t, LANES), F32),
        grid=(t // tm,),
        in_specs=[pl.BlockSpec((tm, D_MODEL), lambda i: (i, 0))],
        out_specs=pl.BlockSpec((tm, LANES), lambda i: (i, 0)),
        compiler_params=_cparams(("parallel",)),
        name="rms",
    )(x2d)


def _window_count(pos, w, length):
    half = w // 2
    return jnp.minimum(pos - half + w, length) - jnp.maximum(pos - half, 0)


def _pool_group(w, grid, x_ref, rs_ref, nw_ref, sc_ref, sh_ref, ga_ref, ps_ref, pb_ref, pw_ref,
                o_ref, colp_ref):
    half = w // 2
    nslab = x_ref.shape[1] // SLAB
    t_i = lax.broadcasted_iota(jnp.int32, (SLAB, SLAB), 0)
    s_i = lax.broadcasted_iota(jnp.int32, (SLAB, SLAB), 1)
    diff = s_i - t_i
    band = (diff >= -half) & (diff <= w - half - 1)
    if grid:
        band = band & ((s_i >> 6) == (t_i >> 6))
        cnt = _window_count(t_i & (GRID_W - 1), w, GRID_W)
    else:
        cnt = _window_count(t_i, w, SLAB)
    sm = jnp.where(band, 1.0, 0.0).astype(BF16)
    inv_cnt = 1.0 / cnt.astype(F32)

    a = nw_ref[...] * (1.0 + sc_ref[0])
    sh = sh_ref[0]
    scale_out = ga_ref[0] * ps_ref[...]
    pw = pw_ref[0]
    pb = pb_ref[...]

    def slab_h(r0):
        x = x_ref[0, pl.ds(r0, SLAB), :]
        rs = rs_ref[0, pl.ds(r0, SLAB), :]
        rs2 = jnp.concatenate([rs, rs], axis=1)
        return x, x * rs2 * a + sh

    def window_sum(h):
        hi = h.astype(BF16)
        lo = (h - hi.astype(F32)).astype(BF16)
        return (jnp.dot(sm, hi, preferred_element_type=F32)
                + jnp.dot(sm, lo, preferred_element_type=F32))

    def finish(r0, x, h, m):
        d = (m - h).astype(BF16)
        mix = jnp.dot(d, pw, preferred_element_type=F32) + pb
        o_ref[0, pl.ds(r0, SLAB), :] = x + scale_out * mix

    if not grid:
        for s in range(nslab):
            x, h = slab_h(s * SLAB)
            finish(s * SLAB, x, h, window_sum(h) * inv_cnt)
        return

    pad = 8 * GRID_W
    zeros = jnp.zeros((pad, POOL_GC), F32)
    colp_ref[pl.ds(0, pad), :] = zeros
    colp_ref[pl.ds(pad + nslab * SLAB, pad), :] = zeros

    def col_body(s, carry):
        r0 = pl.multiple_of(s * SLAB, SLAB)
        _, h = slab_h(r0)
        colp_ref[pl.ds(pad + r0, SLAB), :] = window_sum(h) * inv_cnt
        return carry

    lax.fori_loop(0, nslab, col_body, 0)

    def row_body(s, carry):
        r0 = pl.multiple_of(s * SLAB, SLAB)
        x, h = slab_h(r0)
        acc = colp_ref[pl.ds(pad + r0 - half * GRID_W, SLAB), :]
        for j in range(1, w):
            acc = acc + colp_ref[pl.ds(pad + r0 + (j - half) * GRID_W, SLAB), :]
        row = (r0 + t_i) >> 6
        inv_row = 1.0 / _window_count(row, w, GRID_W).astype(F32)
        finish(r0, x, h, acc * inv_row)
        return carry

    lax.fori_loop(0, nslab, row_body, 0)


def _pool_kernel(x_ref, rs_ref, nw_ref, sc_ref, sh_ref, ga_ref, ps_ref, pb_ref, pw_ref, o_ref,
                 *scratch, grid):
    g = pl.program_id(1)
    colp_ref = scratch[0] if grid else None
    for gi, w in enumerate(POOL_WINDOWS):
        @pl.when(g == gi)
        def _(w=w):
            _pool_group(w, grid, x_ref, rs_ref, nw_ref, sc_ref, sh_ref, ga_ref, ps_ref, pb_ref,
                        pw_ref, o_ref, colp_ref)


def _pool(x3, rsb3, nw, sc, sh, ga, ps, pb, pw_bf16, grid):
    nb, lb, _ = x3.shape
    rows = sc.shape[0]
    row = (lambda b: b) if rows == nb else (lambda b: 0)
    vec = pl.BlockSpec((1, POOL_GC), lambda b, g: (0, g))
    mod = pl.BlockSpec((1, 1, POOL_GC), lambda b, g: (row(b), 0, g))
    scratch = [pltpu.VMEM((lb + 16 * GRID_W, POOL_GC), F32)] if grid else []
    return pl.pallas_call(
        functools.partial(_pool_kernel, grid=grid),
        out_shape=jax.ShapeDtypeStruct(x3.shape, F32),
        grid=(nb, len(POOL_WINDOWS)),
        in_specs=[
            pl.BlockSpec((1, lb, POOL_GC), lambda b, g: (b, 0, g)),
            pl.BlockSpec((1, lb, LANES), lambda b, g: (b, 0, 0)),
            vec, mod, mod, mod, vec, vec,
            pl.BlockSpec((1, POOL_GC, POOL_GC), lambda b, g: (g, 0, 0)),
        ],
        out_specs=pl.BlockSpec((1, lb, POOL_GC), lambda b, g: (b, 0, g)),
        scratch_shapes=scratch,
        compiler_params=_cparams(("parallel", "parallel")),
        name="pool_grid" if grid else "pool_seq",
    )(x3, rsb3, nw, sc, sh, ga, ps, pb, pw_bf16)


def _mlp_kernel(x_ref, nw_ref, sc_ref, sh_ref, g_ref, w1_ref, w2_ref, fw_ref, o_ref, h_ref, acc_ref,
                *, final):
    j = pl.program_id(1)

    @pl.when(j == 0)
    def _():
        h_ref[...] = _norm_mod(x_ref[...], nw_ref[...], sc_ref[0], sh_ref[0]).astype(BF16)
        acc_ref[...] = jnp.zeros_like(acc_ref)

    a = jnp.dot(h_ref[...], w1_ref[...], preferred_element_type=F32)
    a = jnp.maximum(a, 0.0)
    a = (a * a).astype(BF16)
    acc_ref[...] += jnp.dot(a, w2_ref[...], preferred_element_type=F32)

    @pl.when(j == pl.num_programs(1) - 1)
    def _():
        y = x_ref[...] + g_ref[0] * acc_ref[...]
        if final:
            ms = jnp.mean(y * y, axis=-1, keepdims=True)
            y = (y * lax.rsqrt(ms + NORM_EPS)) * fw_ref[...]
        o_ref[...] = y


def _mlp(x2d, nw, sc, sh, g, w1, w2, fw, tm, tf, final):
    t = x2d.shape[0]
    rows = sc.shape[0]
    per_row = t // rows
    vec = pl.BlockSpec((1, D_MODEL), lambda i, j: (0, 0))
    mod = pl.BlockSpec((1, 1, D_MODEL), lambda i, j: ((i * tm) // per_row, 0, 0))
    return pl.pallas_call(
        functools.partial(_mlp_kernel, final=final),
        out_shape=jax.ShapeDtypeStruct((t, D_MODEL), F32),
        grid=(t // tm, D_FF // tf),
        in_specs=[
            pl.BlockSpec((tm, D_MODEL), lambda i, j: (i, 0)),
            vec, mod, mod, mod,
            pl.BlockSpec((D_MODEL, tf), lambda i, j: (0, j)),
            pl.BlockSpec((tf, D_MODEL), lambda i, j: (j, 0)),
            vec,
        ],
        out_specs=pl.BlockSpec((tm, D_MODEL), lambda i, j: (i, 0)),
        scratch_shapes=[pltpu.VMEM((tm, D_MODEL), BF16), pltpu.VMEM((tm, D_MODEL), F32)],
        compiler_params=_cparams(("parallel", "arbitrary")),
        name="mlp",
    )(x2d, nw, sc, sh, g, w1, w2, fw)


def _inproj_kernel(x_ref, nw_ref, sc_ref, sh_ref, w_ref, cos_ref, sin_ref, o_ref, h_ref, *, rope, tn):
    j = pl.program_id(1)

    @pl.when(j == 0)
    def _():
        h_ref[...] = _norm_mod(x_ref[...], nw_ref[...], sc_ref[0], sh_ref[0]).astype(BF16)

    p = jnp.dot(h_ref[...], w_ref[...], preferred_element_type=F32)
    n_qk = 2 * RET_HK // tn
    n_q = RET_HK // tn

    @pl.when(j >= n_qk)
    def _():
        o_ref[...] = p.astype(BF16)

    @pl.when(j < n_qk)
    def _():
        scale = jnp.where(j >= n_q, RET_DK ** -0.5, 1.0).astype(F32)
        for t in range(tn // LANES):
            pt = p[:, t * LANES:(t + 1) * LANES] * scale
            if rope:
                c0 = (t % 2) * LANES
                rot = pltpu.roll(pt, LANES // 2, axis=1)
                pt = pt * cos_ref[:, c0:c0 + LANES] + rot * sin_ref[:, c0:c0 + LANES]
            o_ref[:, t * LANES:(t + 1) * LANES] = pt.astype(BF16)


def _inproj(x2d, nw, sc, sh, w_in, cos, sin, tm, tn, rope):
    t = x2d.shape[0]
    rows = sc.shape[0]
    per_row = t // rows
    n_tab = cos.shape[0] // tm
    vec = pl.BlockSpec((1, D_MODEL), lambda i, j: (0, 0))
    mod = pl.BlockSpec((1, 1, D_MODEL), lambda i, j: ((i * tm) // per_row, 0, 0))
    tab = pl.BlockSpec((tm, RET_DK), lambda i, j: (i % n_tab, 0))
    return pl.pallas_call(
        functools.partial(_inproj_kernel, rope=rope, tn=tn),
        out_shape=jax.ShapeDtypeStruct((t, RET_IN), BF16),
        grid=(t // tm, RET_IN // tn),
        in_specs=[
            pl.BlockSpec((tm, D_MODEL), lambda i, j: (i, 0)),
            vec, mod, mod,
            pl.BlockSpec((D_MODEL, tn), lambda i, j: (0, j)),
            tab, tab,
        ],
        out_specs=pl.BlockSpec((tm, tn), lambda i, j: (i, j)),
        scratch_shapes=[pltpu.VMEM((tm, D_MODEL), BF16)],
        compiler_params=_cparams(("parallel", "arbitrary")),
        name="inproj",
    )(x2d, nw, sc, sh, w_in, cos, sin)


def _log_gamma(decay):
    return jnp.log1p(-jnp.exp2(-jnp.full((1, 1), decay, F32)))


def _dot_tn(a, b):
    return lax.dot_general(a, b, (((0,), (0,)), ((), ())), preferred_element_type=F32)


def _dot_nt(a, b):
    return lax.dot_general(a, b, (((1,), (1,)), ((), ())), preferred_element_type=F32)


def _ret_kernel(dec_ref, q_ref, k_ref, v_ref, g_ref, gnw_ref, *rest, heads, has_state):
    if has_state:
        s0_ref, o_ref, oacc_ref, sf_ref, sb_ref = rest
    else:
        (o_ref,) = rest
    c = SLAB
    nchunks = q_ref.shape[1] // c
    row_k = lax.broadcasted_iota(jnp.int32, (c, RET_DK), 0).astype(F32)
    i_f = lax.broadcasted_iota(jnp.int32, (c, c), 0).astype(F32)
    j_f = lax.broadcasted_iota(jnp.int32, (c, c), 1).astype(F32)
    dif = i_f - j_f

    for hh in range(heads):
        head = pl.program_id(1) * heads + hh
        lgf = _log_gamma(dec_ref[0, head])
        lgb = _log_gamma(dec_ref[1, head])
        dmat = jnp.exp(lgf * jnp.maximum(dif, 0.0) + lgb * jnp.maximum(-dif, 0.0))
        gnw = gnw_ref[head]
        ks = slice(hh * RET_DK, (hh + 1) * RET_DK)
        vs = slice(hh * RET_DV, (hh + 1) * RET_DV)

        def intra(r0):
            qc = q_ref[0, pl.ds(r0, c), ks]
            kc = k_ref[0, pl.ds(r0, c), ks]
            vc = v_ref[0, pl.ds(r0, c), vs]
            p = (_dot_nt(qc, kc) * dmat).astype(BF16)
            return qc, kc, vc, jnp.dot(p, vc, preferred_element_type=F32)

        def finalize(r0, o):
            mu = jnp.mean(o, axis=-1, keepdims=True)
            oc = o - mu
            var = jnp.mean(oc * oc, axis=-1, keepdims=True)
            y = oc * lax.rsqrt(var + GN_EPS) * gnw
            gate = g_ref[0, pl.ds(r0, c), vs].astype(F32)
            o_ref[0, pl.ds(r0, c), vs] = (gate * jax.nn.sigmoid(gate) * y).astype(BF16)

        if not has_state:
            for ci in range(nchunks):
                _, _, _, o = intra(ci * c)
                finalize(ci * c, o)
            continue

        xi_f = jnp.exp(lgf * (row_k + 1.0))
        xi_b = jnp.exp(lgb * (c - row_k))
        zeta_f = jnp.exp(lgf * (c - 1.0 - row_k))
        zeta_b = jnp.exp(lgb * row_k)
        gc_f = jnp.exp(lgf * c)
        gc_b = jnp.exp(lgb * c)
        sf_ref[...] = s0_ref[0, 0, 0, 0]
        sb_ref[...] = s0_ref[0, 0, 1, 0]

        def fwd_body(ci, carry):
            r0 = pl.multiple_of(ci * c, c)
            qc, kc, vc, o = intra(r0)
            qx = (qc.astype(F32) * xi_f).astype(BF16)
            o = o + jnp.dot(qx, sf_ref[...].astype(BF16), preferred_element_type=F32)
            oacc_ref[pl.ds(r0, c), :] = o
            kz = (kc.astype(F32) * zeta_f).astype(BF16)
            sf_ref[...] = gc_f * sf_ref[...] + _dot_tn(kz, vc)
            return carry

        lax.fori_loop(0, nchunks, fwd_body, 0)

        def bwd_body(t, carry):
            r0 = pl.multiple_of((nchunks - 1 - t) * c, c)
            qc = q_ref[0, pl.ds(r0, c), ks]
            kc = k_ref[0, pl.ds(r0, c), ks]
            vc = v_ref[0, pl.ds(r0, c), vs]
            qx = (qc.astype(F32) * xi_b).astype(BF16)
            o = oacc_ref[pl.ds(r0, c), :] + jnp.dot(qx, sb_ref[...].astype(BF16),
                                                    preferred_element_type=F32)
            finalize(r0, o)
            kz = (kc.astype(F32) * zeta_b).astype(BF16)
            sb_ref[...] = gc_b * sb_ref[...] + _dot_tn(kz, vc)
            return carry

        lax.fori_loop(0, nchunks, bwd_body, 0)


def _retcore(qkvg, decay, gnw, state, layer_j, heads):
    b, l, _ = qkvg.shape
    has_state = state is not None
    nh = RET_HEADS // heads
    q_off = 0
    k_off = RET_HK // (RET_DK * heads)
    v_off = 2 * RET_HK // (RET_DV * heads)
    g_off = (2 * RET_HK + RET_HV) // (RET_DV * heads)
    in_specs = [
        pl.BlockSpec(memory_space=pltpu.SMEM),
        pl.BlockSpec((1, l, RET_DK * heads), lambda i, h: (i, 0, q_off + h)),
        pl.BlockSpec((1, l, RET_DK * heads), lambda i, h: (i, 0, k_off + h)),
        pl.BlockSpec((1, l, RET_DV * heads), lambda i, h: (i, 0, v_off + h)),
        pl.BlockSpec((1, l, RET_DV * heads), lambda i, h: (i, 0, g_off + h)),
        pl.BlockSpec((RET_HEADS, 1, RET_DV), lambda i, h: (0, 0, 0)),
    ]
    args = [decay, qkvg, qkvg, qkvg, qkvg, gnw.reshape(RET_HEADS, 1, RET_DV)]
    scratch = []
    if has_state:
        assert heads == 1
        in_specs.append(pl.BlockSpec((1, 1, 2, 1, RET_DK, RET_DV),
                                     lambda i, h: (i, layer_j, 0, h, 0, 0)))
        args.append(state)
        scratch = [pltpu.VMEM((l, RET_DV), F32), pltpu.VMEM((RET_DK, RET_DV), F32),
                   pltpu.VMEM((RET_DK, RET_DV), F32)]
    return pl.pallas_call(
        functools.partial(_ret_kernel, heads=heads, has_state=has_state),
        out_shape=jax.ShapeDtypeStruct((b, l, RET_HV), BF16),
        grid=(b, nh),
        in_specs=in_specs,
        out_specs=pl.BlockSpec((1, l, RET_DV * heads), lambda i, h: (i, 0, h)),
        scratch_shapes=scratch,
        compiler_params=_cparams(("parallel", "parallel")),
        name="retcore_state" if has_state else "retcore",
    )(*args)


def _states_kernel(dec_ref, ka_ref, va_ref, kb_ref, vb_ref, o_ref):
    h = pl.program_id(1)
    j = pl.program_id(2)
    l = ka_ref.shape[1]
    row = lax.broadcasted_iota(jnp.int32, (l, RET_DK), 0).astype(F32)

    def emit(k_ref, v_ref):
        lgf = _log_gamma(dec_ref[j, 0, h])
        lgb = _log_gamma(dec_ref[j, 1, h])
        k = k_ref[0].astype(F32)
        v = v_ref[0]
        kf = (k * jnp.exp(lgf * (l - 1.0 - row))).astype(BF16)
        kb = (k * jnp.exp(lgb * row)).astype(BF16)
        o_ref[0, 0, 0, 0] = _dot_tn(kf, v)
        o_ref[0, 0, 1, 0] = _dot_tn(kb, v)

    @pl.when(j == 0)
    def _():
        emit(ka_ref, va_ref)

    @pl.when(j == 1)
    def _():
        emit(kb_ref, vb_ref)


def _states(qkvg_a, qkvg_b, ret_decay):
    b, l, _ = qkvg_a.shape
    k_off = RET_HK // RET_DK
    v_off = 2 * RET_HK // RET_DV
    kspec = pl.BlockSpec((1, l, RET_DK), lambda i, h, j: (i, 0, k_off + h))
    vspec = pl.BlockSpec((1, l, RET_DV), lambda i, h, j: (i, 0, v_off + h))
    return pl.pallas_call(
        _states_kernel,
        out_shape=jax.ShapeDtypeStruct((b, 2, 2, RET_HEADS, RET_DK, RET_DV), F32),
        grid=(b, RET_HEADS, 2),
        in_specs=[pl.BlockSpec(memory_space=pltpu.SMEM), kspec, vspec, kspec, vspec],
        out_specs=pl.BlockSpec((1, 1, 2, 1, RET_DK, RET_DV), lambda i, h, j: (i, j, 0, h, 0, 0)),
        compiler_params=_cparams(("parallel", "parallel", "arbitrary")),
        name="states",
    )(ret_decay, qkvg_a, qkvg_a, qkvg_b, qkvg_b)


def _outproj_kernel(x_ref, go_ref, w_ref, g_ref, o_ref):
    mix = jnp.dot(go_ref[...], w_ref[...], preferred_element_type=F32)
    o_ref[...] = x_ref[...] + g_ref[0] * mix


def _outproj(x2d, go2d, w_out, ga, tm):
    t = x2d.shape[0]
    rows = ga.shape[0]
    per_row = t // rows
    return pl.pallas_call(
        _outproj_kernel,
        out_shape=jax.ShapeDtypeStruct((t, D_MODEL), F32),
        grid=(t // tm,),
        in_specs=[
            pl.BlockSpec((tm, D_MODEL), lambda i: (i, 0)),
            pl.BlockSpec((tm, RET_HV), lambda i: (i, 0)),
            pl.BlockSpec((RET_HV, D_MODEL), lambda i: (0, 0)),
            pl.BlockSpec((1, 1, D_MODEL), lambda i: ((i * tm) // per_row, 0, 0)),
        ],
        out_specs=pl.BlockSpec((tm, D_MODEL), lambda i: (i, 0)),
        compiler_params=_cparams(("parallel",)),
        name="outproj",
    )(x2d, go2d, w_out, ga)


def _rope_tables(length):
    t = jnp.arange(length)
    row = (t // GRID_W).astype(F32)
    col = (t % GRID_W).astype(F32)
    quarter = RET_DK // 4
    freqs = ROPE_BASE ** (-jnp.arange(quarter, dtype=F32) / quarter)
    sign = jnp.concatenate([-jnp.ones((quarter,), F32), jnp.ones((quarter,), F32)])

    def tab(pos):
        ang = pos[:, None] * freqs[None, :]
        cos = jnp.cos(ang)
        sin = jnp.sin(ang)
        return jnp.concatenate([cos, cos], axis=-1), jnp.concatenate([sin, sin], axis=-1) * sign

    cr, sr = tab(row)
    cc, sc = tab(col)
    return jnp.concatenate([cr, cc], axis=-1), jnp.concatenate([sr, sc], axis=-1)


def _trunk(x, mods, state_ret, grid, p):
    b, l, _ = x.shape
    t = b * l
    tm = 1024
    x2d = x.reshape(t, D_MODEL)
    lb = l if grid else 4 * l
    qkvgs = []
    for i in range(DEPTH):
        j = i // 2
        sh_a, sc_a, g_a, sh_m, sc_m, g_m = (mods[i, k] for k in range(N_MOD))
        nw_a = p["norm_mix_w"][i].reshape(1, D_MODEL)
        if i % 2 == 0:
            rsb = _rms(x2d, tm)
            x3 = _pool(x2d.reshape(t // lb, lb, D_MODEL), rsb.reshape(t // lb, lb, LANES),
                       nw_a, sc_a, sh_a, g_a, p["pool_scale"][j].reshape(1, D_MODEL),
                       p["pool_b"][j].reshape(1, D_MODEL), p["pool_w"][j], grid)
            x2d = x3.reshape(t, D_MODEL)
        else:
            qkvg = _inproj(x2d, nw_a, sc_a, sh_a, p["ret_w_in"][j], p["cos"], p["sin"], tm, 512, grid)
            qkvg3 = qkvg.reshape(b, l, RET_IN)
            qkvgs.append(qkvg3)
            go = _retcore(qkvg3, p["ret_decay"][j], p["ret_gn_w"][j], state_ret, j,
                          1 if grid else RET_HEADS)
            x2d = _outproj(x2d, go.reshape(t, RET_HV), p["ret_w_out"][j], g_a, tm)
        x2d = _mlp(x2d, p["norm_mlp_w"][i].reshape(1, D_MODEL), sc_m, sh_m, g_m, p["mlp_w1"][i],
                   p["mlp_w2"][i], p["final_norm_w"].reshape(1, D_MODEL), tm, 512, i == DEPTH - 1)
    return x2d.reshape(b, l, D_MODEL), qkvgs


def kernel(x_prompt, x_sample, state_ret, c, c_ctx, w_ada, b_ada, norm_mix_w, norm_mlp_w, pool_w,
           pool_b, pool_scale, ret_w_in, ret_decay, ret_gn_w, ret_w_out, mlp_w1, mlp_w2,
           final_norm_w):
    dec_b = c.shape[0]
    cond8 = jnp.concatenate([c_ctx[None, :], c, jnp.zeros((8 - 1 - dec_b, D_MODEL), F32)], axis=0)
    mods = _mods(cond8, w_ada, b_ada)
    mods = mods.reshape(DEPTH, 8, N_MOD, D_MODEL).transpose(0, 2, 1, 3)
    mods_ctx = mods[:, :, 0:1, None, :]
    mods_lat = mods[:, :, 1:1 + dec_b, None, :]

    cos, sin = _rope_tables(x_sample.shape[1])
    p = dict(norm_mix_w=norm_mix_w, norm_mlp_w=norm_mlp_w, pool_w=pool_w.astype(BF16), pool_b=pool_b,
             pool_scale=pool_scale, ret_w_in=ret_w_in.astype(BF16), ret_decay=ret_decay,
             ret_gn_w=ret_gn_w, ret_w_out=ret_w_out.astype(BF16), mlp_w1=mlp_w1.astype(BF16),
             mlp_w2=mlp_w2.astype(BF16), final_norm_w=final_norm_w, cos=cos, sin=sin)

    y_prompt, qkvgs = _trunk(x_prompt, mods_ctx, None, False, p)
    new_state = _states(qkvgs[0], qkvgs[1], ret_decay)
    y_sample, _ = _trunk(x_sample, mods_lat, state_ret, True, p)
    return y_prompt, y_sample, new_state
```

```python
import functools

import jax
import jax.numpy as jnp
from jax import lax
from jax.experimental import pallas as pl
from jax.experimental.pallas import tpu as pltpu

F32 = jnp.float32
BF16 = jnp.bfloat16

D_MODEL = 1024
DEPTH = 4
GRID_W = 64
POOL_WINDOWS = (2, 4, 8, 16)
POOL_GC = 256
RET_HEADS = 4
RET_DK = 256
RET_DV = 512
RET_HK = RET_HEADS * RET_DK
RET_HV = RET_HEADS * RET_DV
RET_IN = 2 * RET_HK + 2 * RET_HV
D_FF = 4 * D_MODEL
ROPE_BASE = 10000.0
NORM_EPS = 1e-6
GN_EPS = 1e-5
N_MOD = 6

LANES = 128
SLAB = 256
VMEM_LIMIT = 56 * 1024 * 1024


def _cparams(sem):
    return pltpu.CompilerParams(dimension_semantics=sem, vmem_limit_bytes=VMEM_LIMIT)


def _norm_mod(x, nw, sc, sh):
    ms = jnp.mean(x * x, axis=-1, keepdims=True)
    return (x * lax.rsqrt(ms + NORM_EPS)) * nw * (1.0 + sc) + sh


def _mods_kernel(c_ref, w_ref, b_ref, o_ref):
    c = c_ref[...]
    s = c * jax.nn.sigmoid(c)
    w = w_ref[0].astype(BF16)
    o_ref[0] = jnp.dot(s.astype(BF16), w, preferred_element_type=F32) + b_ref[0]


def _mods(cond8, w_ada, b_ada):
    tn = 1024
    n = N_MOD * D_MODEL
    return pl.pallas_call(
        _mods_kernel,
        out_shape=jax.ShapeDtypeStruct((DEPTH, 8, n), F32),
        grid=(DEPTH, n // tn),
        in_specs=[
            pl.BlockSpec((8, D_MODEL), lambda l, j: (0, 0)),
            pl.BlockSpec((1, D_MODEL, tn), lambda l, j: (l, 0, j)),
            pl.BlockSpec((1, 1, tn), lambda l, j: (l, 0, j)),
        ],
        out_specs=pl.BlockSpec((1, 8, tn), lambda l, j: (l, 0, j)),
        compiler_params=_cparams(("parallel", "parallel")),
        name="mods",
    )(cond8, w_ada, b_ada.reshape(DEPTH, 1, n))


def _rms_kernel(x_ref, o_ref):
    x = x_ref[...]
    ms = jnp.mean(x * x, axis=-1, keepdims=True)
    o_ref[...] = jnp.broadcast_to(lax.rsqrt(ms + NORM_EPS), o_ref.shape)


def _rms(x2d, tm):
    t = x2d.shape[0]
    return pl.pallas_call(
        _rms_kernel,
        out_shape=jax.ShapeDtypeStruct((t, LANES), F32),
        grid=(t // tm,),
        in_specs=[pl.BlockSpec((tm, D_MODEL), lambda i: (i, 0))],
        out_specs=pl.BlockSpec((tm, LANES), lambda i: (i, 0)),
        compiler_params=_cparams(("parallel",)),
        name="rms",
    )(x2d)


def _window_count(pos, w, length):
    half = w // 2
    return jnp.minimum(pos - half + w, length) - jnp.maximum(pos - half, 0)


def _pool_group(w, grid, x_ref, rs_ref, nw_ref, sc_ref, sh_ref, ga_ref, ps_ref, pb_ref, pw_ref,
                o_ref, colp_ref):
    half = w // 2
    nslab = x_ref.shape[1] // SLAB
    t_i = lax.broadcasted_iota(jnp.int32, (SLAB, SLAB), 0)
    s_i = lax.broadcasted_iota(jnp.int32, (SLAB, SLAB), 1)
    diff = s_i - t_i
    band = (diff >= -half) & (diff <= w - half - 1)
    if grid:
        band = band & ((s_i >> 6) == (t_i >> 6))
        cnt = _window_count(t_i & (GRID_W - 1), w, GRID_W)
    else:
        cnt = _window_count(t_i, w, SLAB)
    sm = jnp.where(band, 1.0, 0.0).astype(BF16)
    inv_cnt = 1.0 / cnt.astype(F32)

    a = nw_ref[...] * (1.0 + sc_ref[0])
    sh = sh_ref[0]
    scale_out = ga_ref[0] * ps_ref[...]
    pw = pw_ref[...]
    pb = pb_ref[...]

    def slab_h(r0):
        x = x_ref[0, pl.ds(r0, SLAB), :]
        rs = rs_ref[0, pl.ds(r0, SLAB), :]
        rs2 = jnp.concatenate([rs, rs], axis=1)
        return x, x * rs2 * a + sh

    def window_sum(h):
        hi = h.astype(BF16)
        lo = (h - hi.astype(F32)).astype(BF16)
        return (jnp.dot(sm, hi, preferred_element_type=F32)
                + jnp.dot(sm, lo, preferred_element_type=F32))

    def finish(r0, x, h, m):
        d = (m - h).astype(BF16)
        mix = jnp.dot(d, pw, preferred_element_type=F32) + pb
        o_ref[0, pl.ds(r0, SLAB), :] = x + scale_out * mix

    if not grid:
        for s in range(nslab):
            x, h = slab_h(s * SLAB)
            finish(s * SLAB, x, h, window_sum(h) * inv_cnt)
        return

    pad = 8 * GRID_W
    zeros = jnp.zeros((pad, POOL_GC), F32)
    colp_ref[pl.ds(0, pad), :] = zeros
    colp_ref[pl.ds(pad + nslab * SLAB, pad), :] = zeros

    def col_body(s, carry):
        r0 = pl.multiple_of(s * SLAB, SLAB)
        _, h = slab_h(r0)
        colp_ref[pl.ds(pad + r0, SLAB), :] = window_sum(h) * inv_cnt
        return carry

    lax.fori_loop(0, nslab, col_body, 0)

    def row_body(s, carry):
        r0 = pl.multiple_of(s * SLAB, SLAB)
        x, h = slab_h(r0)
        acc = colp_ref[pl.ds(pad + r0 - half * GRID_W, SLAB), :]
        for j in range(1, w):
            acc = acc + colp_ref[pl.ds(pad + r0 + (j - half) * GRID_W, SLAB), :]
        row = (r0 + t_i) >> 6
        inv_row = 1.0 / _window_count(row, w, GRID_W).astype(F32)
        finish(r0, x, h, acc * inv_row)
        return carry

    lax.fori_loop(0, nslab, row_body, 0)


def _pool_kernel(x_ref, rs_ref, nw_ref, sc_ref, sh_ref, ga_ref, ps_ref, pb_ref, pw_ref, o_ref,
                 *scratch, grid):
    g = pl.program_id(1)
    colp_ref = scratch[0] if grid else None
    for gi, w in enumerate(POOL_WINDOWS):
        @pl.when(g == gi)
        def _(w=w):
            _pool_group(w, grid, x_ref, rs_ref, nw_ref, sc_ref, sh_ref, ga_ref, ps_ref, pb_ref,
                        pw_ref, o_ref, colp_ref)


def _pool(x3, rsb3, nw, sc, sh, ga, ps, pb, pw_bf16, layer_j, grid):
    nb, lb, _ = x3.shape
    rows = sc.shape[0]
    row = (lambda b: b) if rows == nb else (lambda b: 0)
    vec = pl.BlockSpec((1, POOL_GC), lambda b, g: (0, g))
    mod = pl.BlockSpec((1, 1, POOL_GC), lambda b, g: (row(b), 0, g))
    scratch = [pltpu.VMEM((lb + 16 * GRID_W, POOL_GC), F32)] if grid else []
    return pl.pallas_call(
        functools.partial(_pool_kernel, grid=grid),
        out_shape=jax.ShapeDtypeStruct(x3.shape, F32),
        grid=(nb, len(POOL_WINDOWS)),
        in_specs=[
            pl.BlockSpec((1, lb, POOL_GC), lambda b, g: (b, 0, g)),
            pl.BlockSpec((1, lb, LANES), lambda b, g: (b, 0, 0)),
            vec, mod, mod, mod, vec, vec,
            pl.BlockSpec((None, None, POOL_GC, POOL_GC), lambda b, g: (layer_j, g, 0, 0)),
        ],
        out_specs=pl.BlockSpec((1, lb, POOL_GC), lambda b, g: (b, 0, g)),
        scratch_shapes=scratch,
        compiler_params=_cparams(("parallel", "parallel")),
        name="pool_grid" if grid else "pool_seq",
    )(x3, rsb3, nw, sc, sh, ga, ps, pb, pw_bf16)


def _mlp_kernel(x_ref, nw_ref, sc_ref, sh_ref, g_ref, w1_ref, w2_ref, fw_ref, o_ref, h_ref, acc_ref,
                *, final):
    j = pl.program_id(1)
    nj = pl.num_programs(1)

    def ffn(h):
        a = jnp.dot(h, w1_ref[...], preferred_element_type=F32)
        a = jnp.maximum(a, 0.0)
        return jnp.dot((a * a).astype(BF16), w2_ref[...], preferred_element_type=F32)

    @pl.when(j == 0)
    def _():
        h = _norm_mod(x_ref[...], nw_ref[...], sc_ref[0], sh_ref[0]).astype(BF16)
        h_ref[...] = h
        acc_ref[...] = ffn(h)

    @pl.when((j > 0) & (j < nj - 1))
    def _():
        acc_ref[...] += ffn(h_ref[...])

    @pl.when(j == nj - 1)
    def _():
        y = x_ref[...] + g_ref[0] * (acc_ref[...] + ffn(h_ref[...]))
        if final:
            ms = jnp.mean(y * y, axis=-1, keepdims=True)
            y = (y * lax.rsqrt(ms + NORM_EPS)) * fw_ref[...]
        o_ref[...] = y


def _mlp(x2d, nw, sc, sh, g, w1, w2, fw, layer, tm, tf, final):
    t = x2d.shape[0]
    rows = sc.shape[0]
    per_row = t // rows
    assert D_FF // tf >= 2
    vec = pl.BlockSpec((1, D_MODEL), lambda i, j: (0, 0))
    mod = pl.BlockSpec((1, 1, D_MODEL), lambda i, j: ((i * tm) // per_row, 0, 0))
    return pl.pallas_call(
        functools.partial(_mlp_kernel, final=final),
        out_shape=jax.ShapeDtypeStruct((t, D_MODEL), F32),
        grid=(t // tm, D_FF // tf),
        in_specs=[
            pl.BlockSpec((tm, D_MODEL), lambda i, j: (i, 0)),
            vec, mod, mod, mod,
            pl.BlockSpec((None, D_MODEL, tf), lambda i, j: (layer, 0, j)),
            pl.BlockSpec((None, tf, D_MODEL), lambda i, j: (layer, j, 0)),
            vec,
        ],
        out_specs=pl.BlockSpec((tm, D_MODEL), lambda i, j: (i, 0)),
        scratch_shapes=[pltpu.VMEM((tm, D_MODEL), BF16), pltpu.VMEM((tm, D_MODEL), F32)],
        compiler_params=_cparams(("parallel", "arbitrary")),
        name="mlp",
    )(x2d, nw, sc, sh, g, w1, w2, fw)


def _inproj_kernel(x_ref, nw_ref, sc_ref, sh_ref, w_ref, cos_ref, sin_ref, o_ref, h_ref, *, rope, tc):
    j = pl.program_id(1)
    nchunk = o_ref.shape[1] // tc

    @pl.when(j == 0)
    def _():
        h = _norm_mod(x_ref[...], nw_ref[...], sc_ref[0], sh_ref[0]).astype(BF16)
        h_ref[...] = h
        for c in range(nchunk):
            p = jnp.dot(h, w_ref[:, c * tc:(c + 1) * tc], preferred_element_type=F32)
            if c * tc >= RET_HK:
                p = p * (RET_DK ** -0.5)
            if not rope:
                o_ref[:, c * tc:(c + 1) * tc] = p.astype(BF16)
                continue
            for t in range(tc // LANES):
                pt = p[:, t * LANES:(t + 1) * LANES]
                c0 = (t % 2) * LANES
                rot = pltpu.roll(pt, LANES // 2, axis=1)
                pt = pt * cos_ref[:, c0:c0 + LANES] + rot * sin_ref[:, c0:c0 + LANES]
                o_ref[:, c * tc + t * LANES:c * tc + (t + 1) * LANES] = pt.astype(BF16)

    @pl.when(j > 0)
    def _():
        h = h_ref[...]
        for c in range(nchunk):
            p = jnp.dot(h, w_ref[:, c * tc:(c + 1) * tc], preferred_element_type=F32)
            o_ref[:, c * tc:(c + 1) * tc] = p.astype(BF16)


def _inproj(x2d, nw, sc, sh, w_in, cos, sin, layer_j, tm, rope):
    t = x2d.shape[0]
    rows = sc.shape[0]
    per_row = t // rows
    n_tab = cos.shape[0] // tm
    tn = 2 * RET_HK
    assert RET_HV == tn and RET_IN == 3 * tn
    vec = pl.BlockSpec((1, D_MODEL), lambda i, j: (0, 0))
    mod = pl.BlockSpec((1, 1, D_MODEL), lambda i, j: ((i * tm) // per_row, 0, 0))
    tab = pl.BlockSpec((tm, RET_DK), lambda i, j: (i % n_tab, 0))
    return pl.pallas_call(
        functools.partial(_inproj_kernel, rope=rope, tc=512),
        out_shape=jax.ShapeDtypeStruct((t, RET_IN), BF16),
        grid=(t // tm, RET_IN // tn),
        in_specs=[
            pl.BlockSpec((tm, D_MODEL), lambda i, j: (i, 0)),
            vec, mod, mod,
            pl.BlockSpec((None, D_MODEL, tn), lambda i, j: (layer_j, 0, j)),
            tab, tab,
        ],
        out_specs=pl.BlockSpec((tm, tn), lambda i, j: (i, j)),
        scratch_shapes=[pltpu.VMEM((tm, D_MODEL), BF16)],
        compiler_params=_cparams(("parallel", "arbitrary")),
        name="inproj",
    )(x2d, nw, sc, sh, w_in, cos, sin)


def _log_gamma(decay):
    return jnp.log1p(-jnp.exp2(-jnp.full((1, 1), decay, F32)))


def _dot_tn(a, b):
    return lax.dot_general(a, b, (((0,), (0,)), ((), ())), preferred_element_type=F32)


def _dot_nt(a, b):
    return lax.dot_general(a, b, (((1,), (1,)), ((), ())), preferred_element_type=F32)


def _ret_kernel(dec_ref, q_ref, k_ref, v_ref, g_ref, gnw_ref, *rest, heads, has_state):
    if has_state:
        s0_ref, o_ref, oacc_ref, sf_ref, sb_ref = rest
    else:
        (o_ref,) = rest
    c = SLAB
    nchunks = q_ref.shape[1] // c
    row_k = lax.broadcasted_iota(jnp.int32, (c, RET_DK), 0).astype(F32)
    i_f = lax.broadcasted_iota(jnp.int32, (c, c), 0).astype(F32)
    j_f = lax.broadcasted_iota(jnp.int32, (c, c), 1).astype(F32)
    dif = i_f - j_f

    for hh in range(heads):
        head = pl.program_id(1) * heads + hh
        lgf = _log_gamma(dec_ref[0, head])
        lgb = _log_gamma(dec_ref[1, head])
        dmat = jnp.exp(lgf * jnp.maximum(dif, 0.0) + lgb * jnp.maximum(-dif, 0.0))
        gnw = gnw_ref[head]
        ks = slice(hh * RET_DK, (hh + 1) * RET_DK)
        vs = slice(hh * RET_DV, (hh + 1) * RET_DV)

        def intra(r0):
            qc = q_ref[0, pl.ds(r0, c), ks]
            kc = k_ref[0, pl.ds(r0, c), ks]
            vc = v_ref[0, pl.ds(r0, c), vs]
            p = (_dot_nt(qc, kc) * dmat).astype(BF16)
            return qc, kc, vc, jnp.dot(p, vc, preferred_element_type=F32)

        def finalize(r0, o):
            mu = jnp.mean(o, axis=-1, keepdims=True)
            oc = o - mu
            var = jnp.mean(oc * oc, axis=-1, keepdims=True)
            y = oc * lax.rsqrt(var + GN_EPS) * gnw
            gate = g_ref[0, pl.ds(r0, c), vs].astype(F32)
            o_ref[0, pl.ds(r0, c), vs] = (gate * jax.nn.sigmoid(gate) * y).astype(BF16)

        if not has_state:
            for ci in range(nchunks):
                _, _, _, o = intra(ci * c)
                finalize(ci * c, o)
            continue

        xi_f = jnp.exp(lgf * (row_k + 1.0))
        xi_b = jnp.exp(lgb * (c - row_k))
        zeta_f = jnp.exp(lgf * (c - 1.0 - row_k))
        zeta_b = jnp.exp(lgb * row_k)
        gc_f = jnp.exp(lgf * c)
        gc_b = jnp.exp(lgb * c)
        sf_ref[...] = s0_ref[0, 0, 0, 0]
        sb_ref[...] = s0_ref[0, 0, 1, 0]

        def fwd_body(ci, carry):
            r0 = pl.multiple_of(ci * c, c)
            qc, kc, vc, o = intra(r0)
            qx = (qc.astype(F32) * xi_f).astype(BF16)
            o = o + jnp.dot(qx, sf_ref[...].astype(BF16), preferred_element_type=F32)
            oacc_ref[pl.ds(r0, c), :] = o
            kz = (kc.astype(F32) * zeta_f).astype(BF16)
            sf_ref[...] = gc_f * sf_ref[...] + _dot_tn(kz, vc)
            return carry

        lax.fori_loop(0, nchunks, fwd_body, 0)

        def bwd_body(t, carry):
            r0 = pl.multiple_of((nchunks - 1 - t) * c, c)
            qc = q_ref[0, pl.ds(r0, c), ks]
            kc = k_ref[0, pl.ds(r0, c), ks]
            vc = v_ref[0, pl.ds(r0, c), vs]
            qx = (qc.astype(F32) * xi_b).astype(BF16)
            o = oacc_ref[pl.ds(r0, c), :] + jnp.dot(qx, sb_ref[...].astype(BF16),
                                                    preferred_element_type=F32)
            finalize(r0, o)
            kz = (kc.astype(F32) * zeta_b).astype(BF16)
            sb_ref[...] = gc_b * sb_ref[...] + _dot_tn(kz, vc)
            return carry

        lax.fori_loop(0, nchunks, bwd_body, 0)


def _retcore(qkvg, decay, gnw, state, layer_j, heads):
    b, l, _ = qkvg.shape
    has_state = state is not None
    nh = RET_HEADS // heads
    q_off = 0
    k_off = RET_HK // (RET_DK * heads)
    v_off = 2 * RET_HK // (RET_DV * heads)
    g_off = (2 * RET_HK + RET_HV) // (RET_DV * heads)
    in_specs = [
        pl.BlockSpec(memory_space=pltpu.SMEM),
        pl.BlockSpec((1, l, RET_DK * heads), lambda i, h: (i, 0, q_off + h)),
        pl.BlockSpec((1, l, RET_DK * heads), lambda i, h: (i, 0, k_off + h)),
        pl.BlockSpec((1, l, RET_DV * heads), lambda i, h: (i, 0, v_off + h)),
        pl.BlockSpec((1, l, RET_DV * heads), lambda i, h: (i, 0, g_off + h)),
        pl.BlockSpec((RET_HEADS, 1, RET_DV), lambda i, h: (0, 0, 0)),
    ]
    args = [decay, qkvg, qkvg, qkvg, qkvg, gnw.reshape(RET_HEADS, 1, RET_DV)]
    scratch = []
    if has_state:
        assert heads == 1
        in_specs.append(pl.BlockSpec((1, 1, 2, 1, RET_DK, RET_DV),
                                     lambda i, h: (i, layer_j, 0, h, 0, 0)))
        args.append(state)
        scratch = [pltpu.VMEM((l, RET_DV), F32), pltpu.VMEM((RET_DK, RET_DV), F32),
                   pltpu.VMEM((RET_DK, RET_DV), F32)]
    return pl.pallas_call(
        functools.partial(_ret_kernel, heads=heads, has_state=has_state),
        out_shape=jax.ShapeDtypeStruct((b, l, RET_HV), BF16),
        grid=(b, nh),
        in_specs=in_specs,
        out_specs=pl.BlockSpec((1, l, RET_DV * heads), lambda i, h: (i, 0, h)),
        scratch_shapes=scratch,
        compiler_params=_cparams(("parallel", "parallel")),
        name="retcore_state" if has_state else "retcore",
    )(*args)


def _states_kernel(dec_ref, ka_ref, va_ref, kb_ref, vb_ref, o_ref):
    l = ka_ref.shape[1]
    row = lax.broadcasted_iota(jnp.int32, (l, RET_DK), 0).astype(F32)
    for j, (k_ref, v_ref) in enumerate(((ka_ref, va_ref), (kb_ref, vb_ref))):
        for h in range(RET_HEADS):
            lgf = _log_gamma(dec_ref[j, 0, h])
            lgb = _log_gamma(dec_ref[j, 1, h])
            k = k_ref[0, :, h * RET_DK:(h + 1) * RET_DK].astype(F32)
            v = v_ref[0, :, h * RET_DV:(h + 1) * RET_DV]
            kf = (k * jnp.exp(lgf * (l - 1.0 - row))).astype(BF16)
            kb = (k * jnp.exp(lgb * row)).astype(BF16)
            o_ref[0, j, 0, h] = _dot_tn(kf, v)
            o_ref[0, j, 1, h] = _dot_tn(kb, v)


def _states(qkvg_a, qkvg_b, ret_decay):
    b, l, _ = qkvg_a.shape
    kspec = pl.BlockSpec((1, l, RET_HK), lambda i: (i, 0, 1))
    vspec = pl.BlockSpec((1, l, RET_HV), lambda i: (i, 0, 1))
    return pl.pallas_call(
        _states_kernel,
        out_shape=jax.ShapeDtypeStruct((b, 2, 2, RET_HEADS, RET_DK, RET_DV), F32),
        grid=(b,),
        in_specs=[pl.BlockSpec(memory_space=pltpu.SMEM), kspec, vspec, kspec, vspec],
        out_specs=pl.BlockSpec((1, 2, 2, RET_HEADS, RET_DK, RET_DV), lambda i: (i, 0, 0, 0, 0, 0)),
        compiler_params=_cparams(("parallel",)),
        name="states",
    )(ret_decay, qkvg_a, qkvg_a, qkvg_b, qkvg_b)


def _outproj_kernel(x_ref, go_ref, w_ref, g_ref, o_ref):
    mix = jnp.dot(go_ref[...], w_ref[...], preferred_element_type=F32)
    o_ref[...] = x_ref[...] + g_ref[0] * mix


def _outproj(x2d, go2d, w_out, ga, layer_j, tm):
    t = x2d.shape[0]
    rows = ga.shape[0]
    per_row = t // rows
    return pl.pallas_call(
        _outproj_kernel,
        out_shape=jax.ShapeDtypeStruct((t, D_MODEL), F32),
        grid=(t // tm,),
        in_specs=[
            pl.BlockSpec((tm, D_MODEL), lambda i: (i, 0)),
            pl.BlockSpec((tm, RET_HV), lambda i: (i, 0)),
            pl.BlockSpec((None, RET_HV, D_MODEL), lambda i: (layer_j, 0, 0)),
            pl.BlockSpec((1, 1, D_MODEL), lambda i: ((i * tm) // per_row, 0, 0)),
        ],
        out_specs=pl.BlockSpec((tm, D_MODEL), lambda i: (i, 0)),
        compiler_params=_cparams(("parallel",)),
        name="outproj",
    )(x2d, go2d, w_out, ga)


def _rope_tables(length):
    t = jnp.arange(length)
    row = (t // GRID_W).astype(F32)
    col = (t % GRID_W).astype(F32)
    quarter = RET_DK // 4
    freqs = ROPE_BASE ** (-jnp.arange(quarter, dtype=F32) / quarter)
    sign = jnp.concatenate([-jnp.ones((quarter,), F32), jnp.ones((quarter,), F32)])

    def tab(pos):
        ang = pos[:, None] * freqs[None, :]
        cos = jnp.cos(ang)
        sin = jnp.sin(ang)
        return jnp.concatenate([cos, cos], axis=-1), jnp.concatenate([sin, sin], axis=-1) * sign

    cr, sr = tab(row)
    cc, sc = tab(col)
    return jnp.concatenate([cr, cc], axis=-1), jnp.concatenate([sr, sc], axis=-1)


def _trunk(x, mods, state_ret, grid, p):
    b, l, _ = x.shape
    t = b * l
    tm = 1024
    x2d = x.reshape(t, D_MODEL)
    lb = l if grid else 4 * l
    qkvgs = []
    for i in range(DEPTH):
        j = i // 2
        sh_a, sc_a, g_a, sh_m, sc_m, g_m = (mods[i, k] for k in range(N_MOD))
        nw_a = p["norm_mix_w"][i].reshape(1, D_MODEL)
        if i % 2 == 0:
            rsb = _rms(x2d, tm)
            x3 = _pool(x2d.reshape(t // lb, lb, D_MODEL), rsb.reshape(t // lb, lb, LANES),
                       nw_a, sc_a, sh_a, g_a, p["pool_scale"][j].reshape(1, D_MODEL),
                       p["pool_b"][j].reshape(1, D_MODEL), p["pool_w"], j, grid)
            x2d = x3.reshape(t, D_MODEL)
        else:
            qkvg = _inproj(x2d, nw_a, sc_a, sh_a, p["ret_w_in"], p["cos"], p["sin"], j, tm, grid)
            qkvg3 = qkvg.reshape(b, l, RET_IN)
            qkvgs.append(qkvg3)
            go = _retcore(qkvg3, p["ret_decay"][j], p["ret_gn_w"][j], state_ret, j,
                          1 if grid else RET_HEADS)
            x2d = _outproj(x2d, go.reshape(t, RET_HV), p["ret_w_out"], g_a, j, tm)
        x2d = _mlp(x2d, p["norm_mlp_w"][i].reshape(1, D_MODEL), sc_m, sh_m, g_m, p["mlp_w1"],
                   p["mlp_w2"], p["final_norm_w"].reshape(1, D_MODEL), i, tm, 1024, i == DEPTH - 1)
    return x2d.reshape(b, l, D_MODEL), qkvgs


def kernel(x_prompt, x_sample, state_ret, c, c_ctx, w_ada, b_ada, norm_mix_w, norm_mlp_w, pool_w,
           pool_b, pool_scale, ret_w_in, ret_decay, ret_gn_w, ret_w_out, mlp_w1, mlp_w2,
           final_norm_w):
    dec_b = c.shape[0]
    cond8 = jnp.concatenate([c_ctx[None, :], c, jnp.zeros((8 - 1 - dec_b, D_MODEL), F32)], axis=0)
    mods = _mods(cond8, w_ada, b_ada)
    mods = mods.reshape(DEPTH, 8, N_MOD, D_MODEL).transpose(0, 2, 1, 3)
    mods_ctx = mods[:, :, 0:1, None, :]
    mods_lat = mods[:, :, 1:1 + dec_b, None, :]

    cos, sin = _rope_tables(x_sample.shape[1])
    p = dict(norm_mix_w=norm_mix_w, norm_mlp_w=norm_mlp_w, pool_w=pool_w.astype(BF16), pool_b=pool_b,
             pool_scale=pool_scale, ret_w_in=ret_w_in.astype(BF16), ret_decay=ret_decay,
             ret_gn_w=ret_gn_w, ret_w_out=ret_w_out.astype(BF16), mlp_w1=mlp_w1.astype(BF16),
             mlp_w2=mlp_w2.astype(BF16), final_norm_w=final_norm_w, cos=cos, sin=sin)

    y_prompt, qkvgs = _trunk(x_prompt, mods_ctx, None, False, p)
    new_state = _states(qkvgs[0], qkvgs[1], ret_decay)
    y_sample, _ = _trunk(x_sample, mods_lat, state_ret, True, p)
    return y_prompt, y_sample, new_state
```

```python
import functools

import jax
import jax.numpy as jnp
from jax import lax
from jax.experimental import pallas as pl
from jax.experimental.pallas import tpu as pltpu

F32 = jnp.float32
BF16 = jnp.bfloat16

D_MODEL = 1024
DEPTH = 4
GRID_W = 64
POOL_WINDOWS = (2, 4, 8, 16)
POOL_GC = 256
RET_HEADS = 4
RET_DK = 256
RET_DV = 512
RET_HK = RET_HEADS * RET_DK
RET_HV = RET_HEADS * RET_DV
RET_IN = 2 * RET_HK + 2 * RET_HV
D_FF = 4 * D_MODEL
ROPE_BASE = 10000.0
NORM_EPS = 1e-6
GN_EPS = 1e-5
N_MOD = 6

LANES = 128
SLAB = 256
VMEM_LIMIT = 56 * 1024 * 1024


def _cparams(sem):
    return pltpu.CompilerParams(dimension_semantics=sem, vmem_limit_bytes=VMEM_LIMIT)


def _norm_mod(x, nw, sc, sh):
    ms = jnp.mean(x * x, axis=-1, keepdims=True)
    return (x * lax.rsqrt(ms + NORM_EPS)) * nw * (1.0 + sc) + sh


def _mods_kernel(c_ref, w_ref, b_ref, o_ref):
    c = c_ref[...]
    s = c * jax.nn.sigmoid(c)
    w = w_ref[0].astype(BF16)
    o_ref[0] = jnp.dot(s.astype(BF16), w, preferred_element_type=F32) + b_ref[0]


def _mods(cond8, w_ada, b_ada):
    tn = 1024
    n = N_MOD * D_MODEL
    return pl.pallas_call(
        _mods_kernel,
        out_shape=jax.ShapeDtypeStruct((DEPTH, 8, n), F32),
        grid=(DEPTH, n // tn),
        in_specs=[
            pl.BlockSpec((8, D_MODEL), lambda l, j: (0, 0)),
            pl.BlockSpec((1, D_MODEL, tn), lambda l, j: (l, 0, j)),
            pl.BlockSpec((1, 1, tn), lambda l, j: (l, 0, j)),
        ],
        out_specs=pl.BlockSpec((1, 8, tn), lambda l, j: (l, 0, j)),
        compiler_params=_cparams(("parallel", "parallel")),
        name="mods",
    )(cond8, w_ada, b_ada.reshape(DEPTH, 1, n))


def _rms_kernel(x_ref, o_ref):
    x = x_ref[...]
    ms = jnp.mean(x * x, axis=-1, keepdims=True)
    o_ref[...] = jnp.broadcast_to(lax.rsqrt(ms + NORM_EPS), o_ref.shape)


def _rms(x2d, tm):
    t = x2d.shape[0]
    return pl.pallas_call(
        _rms_kernel,
        out_shape=jax.ShapeDtypeStruct((t, LANES), F32),
        grid=(t // tm,),
        in_specs=[pl.BlockSpec((tm, D_MODEL), lambda i: (i, 0))],
        out_specs=pl.BlockSpec((tm, LANES), lambda i: (i, 0)),
        compiler_params=_cparams(("parallel",)),
        name="rms",
    )(x2d)


def _window_count(pos, w, length):
    half = w // 2
    return jnp.minimum(pos - half + w, length) - jnp.maximum(pos - half, 0)


def _inv_count_table(lb, grid):
    t = jnp.arange(lb)
    rows = []
    for w in POOL_WINDOWS:
        if grid:
            cnt = _window_count(t // GRID_W, w, lb // GRID_W) * _window_count(t % GRID_W, w, GRID_W)
        else:
            cnt = _window_count(t % SLAB, w, SLAB)
        rows.append(1.0 / cnt.astype(F32))
    return jnp.broadcast_to(jnp.stack(rows)[:, :, None], (len(POOL_WINDOWS), lb, LANES))


def _pool_group(w, grid, x_ref, rs_ref, inv_ref, nw_ref, sc_ref, sh_ref, ga_ref, ps_ref, pb_ref,
                pw_ref, o_ref, colp_ref, hbuf_ref):
    half = w // 2
    nslab = x_ref.shape[1] // SLAB
    t_i = lax.broadcasted_iota(jnp.int32, (SLAB, SLAB), 0)
    s_i = lax.broadcasted_iota(jnp.int32, (SLAB, SLAB), 1)
    diff = s_i - t_i
    band = (diff >= -half) & (diff <= w - half - 1)
    if grid:
        band = band & ((s_i >> 6) == (t_i >> 6))
    sm = jnp.where(band, 1.0, 0.0).astype(BF16)

    def inv_cnt(r0):
        v = inv_ref[pl.ds(r0, SLAB), :]
        return jnp.concatenate([v, v], axis=1)

    a = nw_ref[...] * (1.0 + sc_ref[0])
    sh = sh_ref[0]
    scale_out = ga_ref[0] * ps_ref[...]
    pw = pw_ref[...]
    pb = pb_ref[...]

    def slab_h(r0):
        x = x_ref[0, pl.ds(r0, SLAB), :]
        rs = rs_ref[0, pl.ds(r0, SLAB), :]
        rs2 = jnp.concatenate([rs, rs], axis=1)
        return x, x * rs2 * a + sh

    def window_sum(h):
        hi = h.astype(BF16)
        lo = (h - hi.astype(F32)).astype(BF16)
        return (jnp.dot(sm, hi, preferred_element_type=F32)
                + jnp.dot(sm, lo, preferred_element_type=F32))

    def finish(r0, x, h, m):
        d = (m - h).astype(BF16)
        mix = jnp.dot(d, pw, preferred_element_type=F32) + pb
        o_ref[0, pl.ds(r0, SLAB), :] = x + scale_out * mix

    if not grid:
        for s in range(nslab):
            x, h = slab_h(s * SLAB)
            finish(s * SLAB, x, h, window_sum(h) * inv_cnt(s * SLAB))
        return

    pad = 8 * GRID_W
    zeros = jnp.zeros((pad, POOL_GC), F32)
    colp_ref[pl.ds(0, pad), :] = zeros
    colp_ref[pl.ds(pad + nslab * SLAB, pad), :] = zeros

    def col_body(s, carry):
        r0 = pl.multiple_of(s * SLAB, SLAB)
        _, h = slab_h(r0)
        hbuf_ref[pl.ds(r0, SLAB), :] = h
        colp_ref[pl.ds(pad + r0, SLAB), :] = window_sum(h)
        return carry

    lax.fori_loop(0, nslab, col_body, 0, unroll=2)

    def row_body(s, carry):
        r0 = pl.multiple_of(s * SLAB, SLAB)
        acc = colp_ref[pl.ds(pad + r0 - half * GRID_W, SLAB), :]
        for j in range(1, w):
            acc = acc + colp_ref[pl.ds(pad + r0 + (j - half) * GRID_W, SLAB), :]
        finish(r0, x_ref[0, pl.ds(r0, SLAB), :], hbuf_ref[pl.ds(r0, SLAB), :], acc * inv_cnt(r0))
        return carry

    lax.fori_loop(0, nslab, row_body, 0, unroll=2)


def _pool_kernel(x_ref, rs_ref, inv_ref, nw_ref, sc_ref, sh_ref, ga_ref, ps_ref, pb_ref, pw_ref, o_ref,
                 *scratch, grid):
    g = pl.program_id(1)
    colp_ref, hbuf_ref = scratch if grid else (None, None)
    for gi, w in enumerate(POOL_WINDOWS):
        @pl.when(g == gi)
        def _(w=w):
            _pool_group(w, grid, x_ref, rs_ref, inv_ref, nw_ref, sc_ref, sh_ref, ga_ref, ps_ref,
                        pb_ref, pw_ref, o_ref, colp_ref, hbuf_ref)


def _pool(x3, rsb3, nw, sc, sh, ga, ps, pb, pw_bf16, layer_j, grid):
    nb, lb, _ = x3.shape
    rows = sc.shape[0]
    row = (lambda b: b) if rows == nb else (lambda b: 0)
    vec = pl.BlockSpec((1, POOL_GC), lambda b, g: (0, g))
    mod = pl.BlockSpec((1, 1, POOL_GC), lambda b, g: (row(b), 0, g))
    scratch = []
    if grid:
        scratch = [pltpu.VMEM((lb + 16 * GRID_W, POOL_GC), F32), pltpu.VMEM((lb, POOL_GC), F32)]
    return pl.pallas_call(
        functools.partial(_pool_kernel, grid=grid),
        out_shape=jax.ShapeDtypeStruct(x3.shape, F32),
        grid=(nb, len(POOL_WINDOWS)),
        in_specs=[
            pl.BlockSpec((1, lb, POOL_GC), lambda b, g: (b, 0, g)),
            pl.BlockSpec((1, lb, LANES), lambda b, g: (b, 0, 0)),
            pl.BlockSpec((None, lb, LANES), lambda b, g: (g, 0, 0)),
            vec, mod, mod, mod, vec, vec,
            pl.BlockSpec((None, None, POOL_GC, POOL_GC), lambda b, g: (layer_j, g, 0, 0)),
        ],
        out_specs=pl.BlockSpec((1, lb, POOL_GC), lambda b, g: (b, 0, g)),
        scratch_shapes=scratch,
        compiler_params=_cparams(("parallel", "parallel")),
        name="pool_grid" if grid else "pool_seq",
    )(x3, rsb3, _inv_count_table(lb, grid), nw, sc, sh, ga, ps, pb, pw_bf16)


def _mlp_kernel(x_ref, nw_ref, sc_ref, sh_ref, g_ref, w1_ref, w2_ref, fw_ref, o_ref, rs_ref, h_ref,
                acc_ref, *, final, fc):
    j = pl.program_id(1)
    nj = pl.num_programs(1)

    def ffn(h):
        out = None
        for c0 in range(0, w1_ref.shape[1], fc):
            a = jnp.dot(h, w1_ref[:, c0:c0 + fc], preferred_element_type=F32)
            a = jnp.maximum(a, 0.0)
            d = jnp.dot((a * a).astype(BF16), w2_ref[c0:c0 + fc, :], preferred_element_type=F32)
            out = d if out is None else out + d
        return out

    @pl.when(j == 0)
    def _():
        h = _norm_mod(x_ref[...], nw_ref[...], sc_ref[0], sh_ref[0]).astype(BF16)
        h_ref[...] = h
        acc_ref[...] = ffn(h)

    @pl.when((j > 0) & (j < nj - 1))
    def _():
        acc_ref[...] += ffn(h_ref[...])

    @pl.when(j == nj - 1)
    def _():
        y = x_ref[...] + g_ref[0] * (acc_ref[...] + ffn(h_ref[...]))
        rs = lax.rsqrt(jnp.mean(y * y, axis=-1, keepdims=True) + NORM_EPS)
        rs_ref[...] = jnp.broadcast_to(rs, rs_ref.shape)
        if final:
            y = (y * rs) * fw_ref[...]
        o_ref[...] = y


def _mlp(x2d, nw, sc, sh, g, w1, w2, fw, layer, tm, tf, final):
    t = x2d.shape[0]
    rows = sc.shape[0]
    per_row = t // rows
    assert D_FF // tf >= 2
    vec = pl.BlockSpec((1, D_MODEL), lambda i, j: (0, 0))
    mod = pl.BlockSpec((1, 1, D_MODEL), lambda i, j: ((i * tm) // per_row, 0, 0))
    return pl.pallas_call(
        functools.partial(_mlp_kernel, final=final, fc=min(tf, 1024)),
        out_shape=(jax.ShapeDtypeStruct((t, D_MODEL), F32), jax.ShapeDtypeStruct((t, LANES), F32)),
        grid=(t // tm, D_FF // tf),
        in_specs=[
            pl.BlockSpec((tm, D_MODEL), lambda i, j: (i, 0)),
            vec, mod, mod, mod,
            pl.BlockSpec((None, D_MODEL, tf), lambda i, j: (layer, 0, j)),
            pl.BlockSpec((None, tf, D_MODEL), lambda i, j: (layer, j, 0)),
            vec,
        ],
        out_specs=(pl.BlockSpec((tm, D_MODEL), lambda i, j: (i, 0)),
                   pl.BlockSpec((tm, LANES), lambda i, j: (i, 0))),
        scratch_shapes=[pltpu.VMEM((tm, D_MODEL), BF16), pltpu.VMEM((tm, D_MODEL), F32)],
        compiler_params=_cparams(("parallel", "arbitrary")),
        name="mlp",
    )(x2d, nw, sc, sh, g, w1, w2, fw)


def _inproj_kernel(x_ref, nw_ref, sc_ref, sh_ref, w_ref, cos_ref, sin_ref, o_ref, h_ref, *, rope, tc):
    j = pl.program_id(1)
    nchunk = o_ref.shape[1] // tc

    @pl.when(j == 0)
    def _():
        h = _norm_mod(x_ref[...], nw_ref[...], sc_ref[0], sh_ref[0]).astype(BF16)
        h_ref[...] = h
        for c in range(nchunk):
            p = jnp.dot(h, w_ref[:, c * tc:(c + 1) * tc], preferred_element_type=F32)
            if c * tc >= RET_HK:
                p = p * (RET_DK ** -0.5)
            if not rope:
                o_ref[:, c * tc:(c + 1) * tc] = p.astype(BF16)
                continue
            for t in range(tc // LANES):
                pt = p[:, t * LANES:(t + 1) * LANES]
                c0 = (t % 2) * LANES
                rot = pltpu.roll(pt, LANES // 2, axis=1)
                pt = pt * cos_ref[:, c0:c0 + LANES] + rot * sin_ref[:, c0:c0 + LANES]
                o_ref[:, c * tc + t * LANES:c * tc + (t + 1) * LANES] = pt.astype(BF16)

    @pl.when(j > 0)
    def _():
        h = h_ref[...]
        for c in range(nchunk):
            p = jnp.dot(h, w_ref[:, c * tc:(c + 1) * tc], preferred_element_type=F32)
            o_ref[:, c * tc:(c + 1) * tc] = p.astype(BF16)


def _inproj(x2d, nw, sc, sh, w_in, cos, sin, layer_j, tm, rope):
    t = x2d.shape[0]
    rows = sc.shape[0]
    per_row = t // rows
    n_tab = cos.shape[0] // tm
    tn = 2 * RET_HK
    assert RET_HV == tn and RET_IN == 3 * tn
    vec = pl.BlockSpec((1, D_MODEL), lambda i, j: (0, 0))
    mod = pl.BlockSpec((1, 1, D_MODEL), lambda i, j: ((i * tm) // per_row, 0, 0))
    tab = pl.BlockSpec((tm, RET_DK), lambda i, j: (i % n_tab, 0))
    return pl.pallas_call(
        functools.partial(_inproj_kernel, rope=rope, tc=512),
        out_shape=jax.ShapeDtypeStruct((t, RET_IN), BF16),
        grid=(t // tm, RET_IN // tn),
        in_specs=[
            pl.BlockSpec((tm, D_MODEL), lambda i, j: (i, 0)),
            vec, mod, mod,
            pl.BlockSpec((None, D_MODEL, tn), lambda i, j: (layer_j, 0, j)),
            tab, tab,
        ],
        out_specs=pl.BlockSpec((tm, tn), lambda i, j: (i, j)),
        scratch_shapes=[pltpu.VMEM((tm, D_MODEL), BF16)],
        compiler_params=_cparams(("parallel", "arbitrary")),
        name="inproj",
    )(x2d, nw, sc, sh, w_in, cos, sin)


def _log_gamma(decay):
    return jnp.log1p(-jnp.exp2(-jnp.full((1, 1), decay, F32)))


def _dot_tn(a, b):
    return lax.dot_general(a, b, (((0,), (0,)), ((), ())), preferred_element_type=F32)


def _dot_nt(a, b):
    return lax.dot_general(a, b, (((1,), (1,)), ((), ())), preferred_element_type=F32)


def _ret_kernel(dec_ref, q_ref, k_ref, v_ref, g_ref, gnw_ref, *rest, heads, has_state):
    if has_state:
        s0_ref, o_ref, oacc_ref, sf_ref, sb_ref = rest
    else:
        (o_ref,) = rest
    c = SLAB
    nchunks = q_ref.shape[1] // c
    row_k = lax.broadcasted_iota(jnp.int32, (c, RET_DK), 0).astype(F32)
    i_f = lax.broadcasted_iota(jnp.int32, (c, c), 0).astype(F32)
    j_f = lax.broadcasted_iota(jnp.int32, (c, c), 1).astype(F32)
    dif = i_f - j_f

    for hh in range(heads):
        head = pl.program_id(1) * heads + hh
        lgf = _log_gamma(dec_ref[0, head])
        lgb = _log_gamma(dec_ref[1, head])
        dmat = jnp.exp(lgf * jnp.maximum(dif, 0.0) + lgb * jnp.maximum(-dif, 0.0))
        gnw = gnw_ref[head]
        ks = slice(hh * RET_DK, (hh + 1) * RET_DK)
        vs = slice(hh * RET_DV, (hh + 1) * RET_DV)

        def intra(r0):
            qc = q_ref[0, pl.ds(r0, c), ks]
            kc = k_ref[0, pl.ds(r0, c), ks]
            vc = v_ref[0, pl.ds(r0, c), vs]
            p = (_dot_nt(qc, kc) * dmat).astype(BF16)
            return qc, kc, vc, jnp.dot(p, vc, preferred_element_type=F32)

        def finalize(r0, o):
            mu = jnp.mean(o, axis=-1, keepdims=True)
            oc = o - mu
            var = jnp.mean(oc * oc, axis=-1, keepdims=True)
            y = oc * lax.rsqrt(var + GN_EPS) * gnw
            hg = 0.5 * g_ref[0, pl.ds(r0, c), vs].astype(F32)
            o_ref[0, pl.ds(r0, c), vs] = ((hg + hg * jnp.tanh(hg)) * y).astype(BF16)

        if not has_state:
            for ci in range(nchunks):
                _, _, _, o = intra(ci * c)
                finalize(ci * c, o)
            continue

        xi_f = jnp.exp(lgf * (row_k + 1.0))
        xi_b = jnp.exp(lgb * (c - row_k))
        zeta_f = jnp.exp(lgf * (c - 1.0 - row_k))
        zeta_b = jnp.exp(lgb * row_k)
        gc_f = jnp.exp(lgf * c)
        gc_b = jnp.exp(lgb * c)
        sf_ref[...] = s0_ref[0, 0, 0, 0]
        sb_ref[...] = s0_ref[0, 0, 1, 0]

        def fwd_part(r0):
            qc, kc, vc, o = intra(r0)
            qx = (qc.astype(F32) * xi_f).astype(BF16)
            o = o + jnp.dot(qx, sf_ref[...].astype(BF16), preferred_element_type=F32)
            kz = (kc.astype(F32) * zeta_f).astype(BF16)
            sf_ref[...] = gc_f * sf_ref[...] + _dot_tn(kz, vc)
            return o

        def bwd_part(r0):
            qc = q_ref[0, pl.ds(r0, c), ks]
            kc = k_ref[0, pl.ds(r0, c), ks]
            vc = v_ref[0, pl.ds(r0, c), vs]
            qx = (qc.astype(F32) * xi_b).astype(BF16)
            o = jnp.dot(qx, sb_ref[...].astype(BF16), preferred_element_type=F32)
            kz = (kc.astype(F32) * zeta_b).astype(BF16)
            sb_ref[...] = gc_b * sb_ref[...] + _dot_tn(kz, vc)
            return o

        def first_half(t, carry):
            rf = pl.multiple_of(t * c, c)
            rb = pl.multiple_of((nchunks - 1 - t) * c, c)
            oacc_ref[pl.ds(rf, c), :] = fwd_part(rf)
            oacc_ref[pl.ds(rb, c), :] = bwd_part(rb)
            return carry

        def second_half(t, carry):
            rf = pl.multiple_of(t * c, c)
            rb = pl.multiple_of((nchunks - 1 - t) * c, c)
            finalize(rf, oacc_ref[pl.ds(rf, c), :] + fwd_part(rf))
            finalize(rb, oacc_ref[pl.ds(rb, c), :] + bwd_part(rb))
            return carry

        assert nchunks % 2 == 0
        lax.fori_loop(0, nchunks // 2, first_half, 0)
        lax.fori_loop(nchunks // 2, nchunks, second_half, 0)


def _retcore(qkvg, decay, gnw, state, layer_j, heads):
    b, l, _ = qkvg.shape
    has_state = state is not None
    nh = RET_HEADS // heads
    q_off = 0
    k_off = RET_HK // (RET_DK * heads)
    v_off = 2 * RET_HK // (RET_DV * heads)
    g_off = (2 * RET_HK + RET_HV) // (RET_DV * heads)
    in_specs = [
        pl.BlockSpec(memory_space=pltpu.SMEM),
        pl.BlockSpec((1, l, RET_DK * heads), lambda i, h: (i, 0, q_off + h)),
        pl.BlockSpec((1, l, RET_DK * heads), lambda i, h: (i, 0, k_off + h)),
        pl.BlockSpec((1, l, RET_DV * heads), lambda i, h: (i, 0, v_off + h)),
        pl.BlockSpec((1, l, RET_DV * heads), lambda i, h: (i, 0, g_off + h)),
        pl.BlockSpec((RET_HEADS, 1, RET_DV), lambda i, h: (0, 0, 0)),
    ]
    args = [decay, qkvg, qkvg, qkvg, qkvg, gnw.reshape(RET_HEADS, 1, RET_DV)]
    scratch = []
    if has_state:
        assert heads == 1
        in_specs.append(pl.BlockSpec((1, 1, 2, 1, RET_DK, RET_DV),
                                     lambda i, h: (i, layer_j, 0, h, 0, 0)))
        args.append(state)
        scratch = [pltpu.VMEM((l, RET_DV), F32), pltpu.VMEM((RET_DK, RET_DV), F32),
                   pltpu.VMEM((RET_DK, RET_DV), F32)]
    return pl.pallas_call(
        functools.partial(_ret_kernel, heads=heads, has_state=has_state),
        out_shape=jax.ShapeDtypeStruct((b, l, RET_HV), BF16),
        grid=(b, nh),
        in_specs=in_specs,
        out_specs=pl.BlockSpec((1, l, RET_DV * heads), lambda i, h: (i, 0, h)),
        scratch_shapes=scratch,
        compiler_params=_cparams(("parallel", "parallel")),
        name="retcore_state" if has_state else "retcore",
    )(*args)


def _states_kernel(dec_ref, ka_ref, va_ref, kb_ref, vb_ref, o_ref):
    l = ka_ref.shape[1]
    row = lax.broadcasted_iota(jnp.int32, (l, RET_DK), 0).astype(F32)
    for j, (k_ref, v_ref) in enumerate(((ka_ref, va_ref), (kb_ref, vb_ref))):
        for h in range(RET_HEADS):
            lgf = _log_gamma(dec_ref[j, 0, h])
            lgb = _log_gamma(dec_ref[j, 1, h])
            k = k_ref[0, :, h * RET_DK:(h + 1) * RET_DK].astype(F32)
            v = v_ref[0, :, h * RET_DV:(h + 1) * RET_DV]
            kf = (k * jnp.exp(lgf * (l - 1.0 - row))).astype(BF16)
            kb = (k * jnp.exp(lgb * row)).astype(BF16)
            o_ref[0, j, 0, h] = _dot_tn(kf, v)
            o_ref[0, j, 1, h] = _dot_tn(kb, v)


def _states(qkvg_a, qkvg_b, ret_decay):
    b, l, _ = qkvg_a.shape
    kspec = pl.BlockSpec((1, l, RET_HK), lambda i: (i, 0, 1))
    vspec = pl.BlockSpec((1, l, RET_HV), lambda i: (i, 0, 1))
    return pl.pallas_call(
        _states_kernel,
        out_shape=jax.ShapeDtypeStruct((b, 2, 2, RET_HEADS, RET_DK, RET_DV), F32),
        grid=(b,),
        in_specs=[pl.BlockSpec(memory_space=pltpu.SMEM), kspec, vspec, kspec, vspec],
        out_specs=pl.BlockSpec((1, 2, 2, RET_HEADS, RET_DK, RET_DV), lambda i: (i, 0, 0, 0, 0, 0)),
        compiler_params=_cparams(("parallel",)),
        name="states",
    )(ret_decay, qkvg_a, qkvg_a, qkvg_b, qkvg_b)


def _outproj_kernel(x_ref, go_ref, w_ref, g_ref, o_ref):
    mix = jnp.dot(go_ref[...], w_ref[...], preferred_element_type=F32)
    o_ref[...] = x_ref[...] + g_ref[0] * mix


def _outproj(x2d, go2d, w_out, ga, layer_j, tm):
    t = x2d.shape[0]
    rows = ga.shape[0]
    per_row = t // rows
    return pl.pallas_call(
        _outproj_kernel,
        out_shape=jax.ShapeDtypeStruct((t, D_MODEL), F32),
        grid=(t // tm,),
        in_specs=[
            pl.BlockSpec((tm, D_MODEL), lambda i: (i, 0)),
            pl.BlockSpec((tm, RET_HV), lambda i: (i, 0)),
            pl.BlockSpec((None, RET_HV, D_MODEL), lambda i: (layer_j, 0, 0)),
            pl.BlockSpec((1, 1, D_MODEL), lambda i: ((i * tm) // per_row, 0, 0)),
        ],
        out_specs=pl.BlockSpec((tm, D_MODEL), lambda i: (i, 0)),
        compiler_params=_cparams(("parallel",)),
        name="outproj",
    )(x2d, go2d, w_out, ga)


def _rope_tables(length):
    quarter = RET_DK // 4
    half = RET_DK // 2
    rows = length // GRID_W
    freqs = ROPE_BASE ** (-jnp.arange(quarter, dtype=F32) / quarter)
    sign = jnp.concatenate([-jnp.ones((quarter,), F32), jnp.ones((quarter,), F32)])

    def tabs(npos):
        ang = jnp.arange(npos, dtype=F32)[:, None] * freqs[None, :]
        return (jnp.concatenate([jnp.cos(ang), jnp.cos(ang)], axis=-1),
                jnp.concatenate([jnp.sin(ang), jnp.sin(ang)], axis=-1) * sign)

    def by_row(tab):
        return jnp.broadcast_to(tab[:, None, :], (rows, GRID_W, half)).reshape(length, half)

    def by_col(tab):
        return jnp.broadcast_to(tab[None, :, :], (rows, GRID_W, half)).reshape(length, half)

    cos_r, sin_r = tabs(rows)
    cos_c, sin_c = tabs(GRID_W)
    return (jnp.concatenate([by_row(cos_r), by_col(cos_c)], axis=-1),
            jnp.concatenate([by_row(sin_r), by_col(sin_c)], axis=-1))


def _trunk(x, mods, state_ret, grid, p):
    b, l, _ = x.shape
    t = b * l
    tm = 1024
    x2d = x.reshape(t, D_MODEL)
    lb = l if grid else 4 * l
    qkvgs = []
    rsb = _rms(x2d, tm)
    for i in range(DEPTH):
        j = i // 2
        sh_a, sc_a, g_a, sh_m, sc_m, g_m = (mods[i, k] for k in range(N_MOD))
        nw_a = p["norm_mix_w"][i].reshape(1, D_MODEL)
        if i % 2 == 0:
            x3 = _pool(x2d.reshape(t // lb, lb, D_MODEL), rsb.reshape(t // lb, lb, LANES),
                       nw_a, sc_a, sh_a, g_a, p["pool_scale"][j].reshape(1, D_MODEL),
                       p["pool_b"][j].reshape(1, D_MODEL), p["pool_w"], j, grid)
            x2d = x3.reshape(t, D_MODEL)
        else:
            qkvg = _inproj(x2d, nw_a, sc_a, sh_a, p["ret_w_in"], p["cos"], p["sin"], j, tm, grid)
            qkvg3 = qkvg.reshape(b, l, RET_IN)
            qkvgs.append(qkvg3)
            go = _retcore(qkvg3, p["ret_decay"][j], p["ret_gn_w"][j], state_ret, j,
                          1 if grid else RET_HEADS)
            x2d = _outproj(x2d, go.reshape(t, RET_HV), p["ret_w_out"], g_a, j, tm)
        x2d, rsb = _mlp(x2d, p["norm_mlp_w"][i].reshape(1, D_MODEL), sc_m, sh_m, g_m, p["mlp_w1"],
                        p["mlp_w2"], p["final_norm_w"].reshape(1, D_MODEL), i, tm, 2048,
                        i == DEPTH - 1)
    return x2d.reshape(b, l, D_MODEL), qkvgs


def kernel(x_prompt, x_sample, state_ret, c, c_ctx, w_ada, b_ada, norm_mix_w, norm_mlp_w, pool_w,
           pool_b, pool_scale, ret_w_in, ret_decay, ret_gn_w, ret_w_out, mlp_w1, mlp_w2,
           final_norm_w):
    dec_b = c.shape[0]
    cond8 = jnp.concatenate([c_ctx[None, :], c, jnp.zeros((8 - 1 - dec_b, D_MODEL), F32)], axis=0)
    mods = _mods(cond8, w_ada, b_ada)
    mods = mods.reshape(DEPTH, 8, N_MOD, D_MODEL).transpose(0, 2, 1, 3)
    mods_ctx = mods[:, :, 0:1, None, :]
    mods_lat = mods[:, :, 1:1 + dec_b, None, :]

    cos, sin = _rope_tables(x_sample.shape[1])
    p = dict(norm_mix_w=norm_mix_w, norm_mlp_w=norm_mlp_w, pool_w=pool_w.astype(BF16), pool_b=pool_b,
             pool_scale=pool_scale, ret_w_in=ret_w_in.astype(BF16), ret_decay=ret_decay,
             ret_gn_w=ret_gn_w, ret_w_out=ret_w_out.astype(BF16), mlp_w1=mlp_w1.astype(BF16),
             mlp_w2=mlp_w2.astype(BF16), final_norm_w=final_norm_w, cos=cos, sin=sin)

    y_prompt, qkvgs = _trunk(x_prompt, mods_ctx, None, False, p)
    new_state = _states(qkvgs[0], qkvgs[1], ret_decay)
    y_sample, _ = _trunk(x_sample, mods_lat, state_ret, True, p)
    return y_prompt, y_sample, new_state
```

```python
import functools

import jax
import jax.numpy as jnp
from jax import lax
from jax.experimental import pallas as pl
from jax.experimental.pallas import tpu as pltpu

F32 = jnp.float32
BF16 = jnp.bfloat16

D_MODEL = 1024
DEPTH = 4
GRID_W = 64
POOL_WINDOWS = (2, 4, 8, 16)
POOL_GC = 256
RET_HEADS = 4
RET_DK = 256
RET_DV = 512
RET_HK = RET_HEADS * RET_DK
RET_HV = RET_HEADS * RET_DV
RET_IN = 2 * RET_HK + 2 * RET_HV
D_FF = 4 * D_MODEL
ROPE_BASE = 10000.0
NORM_EPS = 1e-6
GN_EPS = 1e-5
N_MOD = 6

LANES = 128
SLAB = 256
TM = 1024
VMEM_LIMIT = 56 * 1024 * 1024


def _cparams(sem):
    return pltpu.CompilerParams(dimension_semantics=sem, vmem_limit_bytes=VMEM_LIMIT)


def _norm_mod(x, nw, sc, sh):
    ms = jnp.mean(x * x, axis=-1, keepdims=True)
    return (x * lax.rsqrt(ms + NORM_EPS)) * nw * (1.0 + sc) + sh


def _rsqrt_ms_lanes(x):
    rs = lax.rsqrt(jnp.mean(x * x, axis=-1, keepdims=True) + NORM_EPS)
    return jnp.broadcast_to(rs, (x.shape[0], LANES))


class _Tokens:
    def __init__(self, nc, lat, nlat):
        assert nc % TM == 0 and lat % TM == 0
        self.nc, self.lat, self.nlat = nc, lat, nlat
        self.t = nc + lat * nlat
        self.n_ctx = nc // TM
        self.nt = self.t // TM

    def mod_row(self, i):
        return jnp.where(i < self.n_ctx, 0, 1 + (i - self.n_ctx) // (self.lat // TM))

    def ctx_tile(self, i):
        return jnp.minimum(i, self.n_ctx - 1)

    def lat_tile(self, i):
        return jnp.maximum(i - self.n_ctx, 0)


def _mods_kernel(c_ref, w_ref, b_ref, o_ref):
    c = c_ref[...]
    s = c * jax.nn.sigmoid(c)
    w = w_ref[0].astype(BF16)
    o_ref[0] = jnp.dot(s.astype(BF16), w, preferred_element_type=F32) + b_ref[0]


def _mods(cond8, w_ada, b_ada):
    tn = 1024
    n = N_MOD * D_MODEL
    return pl.pallas_call(
        _mods_kernel,
        out_shape=jax.ShapeDtypeStruct((DEPTH, 8, n), F32),
        grid=(DEPTH, n // tn),
        in_specs=[
            pl.BlockSpec((8, D_MODEL), lambda l, j: (0, 0)),
            pl.BlockSpec((1, D_MODEL, tn), lambda l, j: (l, 0, j)),
            pl.BlockSpec((1, 1, tn), lambda l, j: (l, 0, j)),
        ],
        out_specs=pl.BlockSpec((1, 8, tn), lambda l, j: (l, 0, j)),
        compiler_params=_cparams(("parallel", "parallel")),
        name="mods",
    )(cond8, w_ada, b_ada.reshape(DEPTH, 1, n))


def _pack_kernel(xc_ref, xl_ref, o_ref, rs_ref, *, n_ctx):
    i = pl.program_id(0)

    def emit(x_ref):
        x = x_ref[...]
        o_ref[...] = x
        rs_ref[...] = _rsqrt_ms_lanes(x)

    pl.when(i < n_ctx)(lambda: emit(xc_ref))
    pl.when(i >= n_ctx)(lambda: emit(xl_ref))


def _pack(tk, xc, xl):
    return pl.pallas_call(
        functools.partial(_pack_kernel, n_ctx=tk.n_ctx),
        out_shape=(jax.ShapeDtypeStruct((tk.t, D_MODEL), F32), jax.ShapeDtypeStruct((tk.t, LANES), F32)),
        grid=(tk.nt,),
        in_specs=[pl.BlockSpec((TM, D_MODEL), lambda i: (tk.ctx_tile(i), 0)),
                  pl.BlockSpec((TM, D_MODEL), lambda i: (tk.lat_tile(i), 0))],
        out_specs=(pl.BlockSpec((TM, D_MODEL), lambda i: (i, 0)),
                   pl.BlockSpec((TM, LANES), lambda i: (i, 0))),
        compiler_params=_cparams(("arbitrary",)),
        name="pack",
    )(xc, xl)


def _window_count(pos, w, length):
    half = w // 2
    return jnp.minimum(pos - half + w, length) - jnp.maximum(pos - half, 0)


def _inv_count_table(lb, grid):
    t = jnp.arange(lb)
    rows = []
    for w in POOL_WINDOWS:
        if grid:
            cnt = _window_count(t // GRID_W, w, lb // GRID_W) * _window_count(t % GRID_W, w, GRID_W)
        else:
            cnt = _window_count(t % SLAB, w, SLAB)
        rows.append(1.0 / cnt.astype(F32))
    return jnp.broadcast_to(jnp.stack(rows)[:, :, None], (len(POOL_WINDOWS), lb, LANES))


def _pool_group(w, grid, x_ref, rs_ref, inv_ref, nw_ref, sc_ref, sh_ref, ga_ref, ps_ref, pb_ref,
                pw_ref, o_ref, colp_ref, hbuf_ref):
    half = w // 2
    nslab = x_ref.shape[1] // SLAB
    t_i = lax.broadcasted_iota(jnp.int32, (SLAB, SLAB), 0)
    s_i = lax.broadcasted_iota(jnp.int32, (SLAB, SLAB), 1)
    diff = s_i - t_i
    band = (diff >= -half) & (diff <= w - half - 1)
    if grid:
        band = band & ((s_i >> 6) == (t_i >> 6))
    sm = jnp.where(band, 1.0, 0.0).astype(BF16)

    def inv_cnt(r0):
        v = inv_ref[pl.ds(r0, SLAB), :]
        return jnp.concatenate([v, v], axis=1)

    a = nw_ref[...] * (1.0 + sc_ref[0])
    sh = sh_ref[0]
    scale_out = ga_ref[0] * ps_ref[...]
    pw = pw_ref[...]
    pb = pb_ref[...]

    def slab_h(r0):
        x = x_ref[0, pl.ds(r0, SLAB), :]
        rs = rs_ref[0, pl.ds(r0, SLAB), :]
        rs2 = jnp.concatenate([rs, rs], axis=1)
        return x, x * rs2 * a + sh

    def window_sum(h):
        hi = h.astype(BF16)
        lo = (h - hi.astype(F32)).astype(BF16)
        return (jnp.dot(sm, hi, preferred_element_type=F32)
                + jnp.dot(sm, lo, preferred_element_type=F32))

    def finish(r0, x, h, m):
        d = (m - h).astype(BF16)
        mix = jnp.dot(d, pw, preferred_element_type=F32) + pb
        o_ref[0, pl.ds(r0, SLAB), :] = x + scale_out * mix

    if not grid:
        def seq_body(s, carry):
            r0 = pl.multiple_of(s * SLAB, SLAB)
            x, h = slab_h(r0)
            finish(r0, x, h, window_sum(h) * inv_cnt(r0))
            return carry

        lax.fori_loop(0, nslab, seq_body, 0, unroll=2)
        return

    pad = 8 * GRID_W
    zeros = jnp.zeros((pad, POOL_GC), F32)
    colp_ref[pl.ds(0, pad), :] = zeros
    colp_ref[pl.ds(pad + nslab * SLAB, pad), :] = zeros

    def col_body(s, carry):
        r0 = pl.multiple_of(s * SLAB, SLAB)
        _, h = slab_h(r0)
        hbuf_ref[pl.ds(r0, SLAB), :] = h
        colp_ref[pl.ds(pad + r0, SLAB), :] = window_sum(h)
        return carry

    lax.fori_loop(0, nslab, col_body, 0, unroll=2)

    def row_body(s, carry):
        r0 = pl.multiple_of(s * SLAB, SLAB)
        acc = colp_ref[pl.ds(pad + r0 - half * GRID_W, SLAB), :]
        for j in range(1, w):
            acc = acc + colp_ref[pl.ds(pad + r0 + (j - half) * GRID_W, SLAB), :]
        finish(r0, x_ref[0, pl.ds(r0, SLAB), :], hbuf_ref[pl.ds(r0, SLAB), :], acc * inv_cnt(r0))
        return carry

    lax.fori_loop(0, nslab, row_body, 0, unroll=2)


def _pool_kernel(x_ref, rs_ref, inv_ref, nw_ref, sc_ref, sh_ref, ga_ref, ps_ref, pb_ref, pw_ref, o_ref,
                 colp_ref, hbuf_ref):
    b = pl.program_id(0)
    g = pl.program_id(1)
    for gi, w in enumerate(POOL_WINDOWS):
        for grid in (False, True):
            @pl.when((g == gi) & ((b > 0) if grid else (b == 0)))
            def _(w=w, grid=grid):
                _pool_group(w, grid, x_ref, rs_ref, inv_ref, nw_ref, sc_ref, sh_ref, ga_ref, ps_ref,
                            pb_ref, pw_ref, o_ref, colp_ref, hbuf_ref)


def _pool(x3, rsb3, inv, nw, sc, sh, ga, ps, pb, pw_bf16, layer_j):
    nb, lb, _ = x3.shape
    vec = pl.BlockSpec((1, POOL_GC), lambda b, g: (0, g))
    mod = pl.BlockSpec((1, 1, POOL_GC), lambda b, g: (b, 0, g))
    return pl.pallas_call(
        _pool_kernel,
        out_shape=jax.ShapeDtypeStruct(x3.shape, F32),
        grid=(nb, len(POOL_WINDOWS)),
        in_specs=[
            pl.BlockSpec((1, lb, POOL_GC), lambda b, g: (b, 0, g)),
            pl.BlockSpec((1, lb, LANES), lambda b, g: (b, 0, 0)),
            pl.BlockSpec((None, None, lb, LANES), lambda b, g: (jnp.minimum(b, 1), g, 0, 0)),
            vec, mod, mod, mod, vec, vec,
            pl.BlockSpec((None, None, POOL_GC, POOL_GC), lambda b, g: (layer_j, g, 0, 0)),
        ],
        out_specs=pl.BlockSpec((1, lb, POOL_GC), lambda b, g: (b, 0, g)),
        scratch_shapes=[pltpu.VMEM((lb + 16 * GRID_W, POOL_GC), F32), pltpu.VMEM((lb, POOL_GC), F32)],
        compiler_params=_cparams(("parallel", "parallel")),
        name="pool",
    )(x3, rsb3, inv, nw, sc, sh, ga, ps, pb, pw_bf16)


def _mlp_kernel(x_ref, nw_ref, sc_ref, sh_ref, g_ref, w1_ref, w2_ref, fw_ref, *rest, final, fc, n_ctx):
    if final:
        yc_ref, yl_ref, h_ref, acc_ref = rest
    else:
        o_ref, rs_ref, h_ref, acc_ref = rest
    i = pl.program_id(0)
    j = pl.program_id(1)
    nj = pl.num_programs(1)

    def ffn(h):
        out = None
        for c0 in range(0, w1_ref.shape[1], fc):
            a = jnp.dot(h, w1_ref[:, c0:c0 + fc], preferred_element_type=F32)
            a = jnp.maximum(a, 0.0)
            d = jnp.dot((a * a).astype(BF16), w2_ref[c0:c0 + fc, :], preferred_element_type=F32)
            out = d if out is None else out + d
        return out

    @pl.when(j == 0)
    def _():
        h = _norm_mod(x_ref[...], nw_ref[...], sc_ref[0], sh_ref[0]).astype(BF16)
        h_ref[...] = h
        acc_ref[...] = ffn(h)

    @pl.when((j > 0) & (j < nj - 1))
    def _():
        acc_ref[...] += ffn(h_ref[...])

    @pl.when(j == nj - 1)
    def _():
        y = x_ref[...] + g_ref[0] * (acc_ref[...] + ffn(h_ref[...]))
        if not final:
            o_ref[...] = y
            rs_ref[...] = _rsqrt_ms_lanes(y)
            return
        rs = lax.rsqrt(jnp.mean(y * y, axis=-1, keepdims=True) + NORM_EPS)
        y = (y * rs) * fw_ref[...]

        @pl.when(i < n_ctx)
        def _():
            yc_ref[...] = y

        @pl.when(i >= n_ctx)
        def _():
            yl_ref[...] = y


def _mlp(tk, x2d, nw, sc, sh, g, w1, w2, fw, layer, tf, final):
    assert D_FF // tf >= 2
    vec = pl.BlockSpec((1, D_MODEL), lambda i, j: (0, 0))
    mod = pl.BlockSpec((1, 1, D_MODEL), lambda i, j: (tk.mod_row(i), 0, 0))
    if final:
        out_shape = (jax.ShapeDtypeStruct((tk.nc, D_MODEL), F32),
                     jax.ShapeDtypeStruct((tk.t - tk.nc, D_MODEL), F32))
        out_specs = (pl.BlockSpec((TM, D_MODEL), lambda i, j: (tk.ctx_tile(i), 0)),
                     pl.BlockSpec((TM, D_MODEL), lambda i, j: (tk.lat_tile(i), 0)))
    else:
        out_shape = (jax.ShapeDtypeStruct((tk.t, D_MODEL), F32), jax.ShapeDtypeStruct((tk.t, LANES), F32))
        out_specs = (pl.BlockSpec((TM, D_MODEL), lambda i, j: (i, 0)),
                     pl.BlockSpec((TM, LANES), lambda i, j: (i, 0)))
    return pl.pallas_call(
        functools.partial(_mlp_kernel, final=final, fc=min(tf, 1024), n_ctx=tk.n_ctx),
        out_shape=out_shape,
        grid=(tk.nt, D_FF // tf),
        in_specs=[
            pl.BlockSpec((TM, D_MODEL), lambda i, j: (i, 0)),
            vec, mod, mod, mod,
            pl.BlockSpec((None, D_MODEL, tf), lambda i, j: (layer, 0, j)),
            pl.BlockSpec((None, tf, D_MODEL), lambda i, j: (layer, j, 0)),
            vec,
        ],
        out_specs=out_specs,
        scratch_shapes=[pltpu.VMEM((TM, D_MODEL), BF16), pltpu.VMEM((TM, D_MODEL), F32)],
        compiler_params=_cparams(("arbitrary", "arbitrary")),
        name="mlp",
    )(x2d, nw, sc, sh, g, w1, w2, fw)


def _inproj_kernel(x_ref, nw_ref, sc_ref, sh_ref, w_ref, cos_ref, sin_ref, o_ref, h_ref, *, tc, n_ctx):
    i = pl.program_id(0)
    j = pl.program_id(1)
    nchunk = o_ref.shape[1] // tc

    def qk_step(rope):
        h = _norm_mod(x_ref[...], nw_ref[...], sc_ref[0], sh_ref[0]).astype(BF16)
        h_ref[...] = h
        for c in range(nchunk):
            p = jnp.dot(h, w_ref[:, c * tc:(c + 1) * tc], preferred_element_type=F32)
            if c * tc >= RET_HK:
                p = p * (RET_DK ** -0.5)
            if not rope:
                o_ref[:, c * tc:(c + 1) * tc] = p.astype(BF16)
                continue
            for t in range(tc // LANES):
                pt = p[:, t * LANES:(t + 1) * LANES]
                c0 = (t % 2) * LANES
                rot = pltpu.roll(pt, LANES // 2, axis=1)
                pt = pt * cos_ref[:, c0:c0 + LANES] + rot * sin_ref[:, c0:c0 + LANES]
                o_ref[:, c * tc + t * LANES:c * tc + (t + 1) * LANES] = pt.astype(BF16)

    pl.when((j == 0) & (i < n_ctx))(lambda: qk_step(False))
    pl.when((j == 0) & (i >= n_ctx))(lambda: qk_step(True))

    @pl.when(j > 0)
    def _():
        h = h_ref[...]
        for c in range(nchunk):
            p = jnp.dot(h, w_ref[:, c * tc:(c + 1) * tc], preferred_element_type=F32)
            o_ref[:, c * tc:(c + 1) * tc] = p.astype(BF16)


def _inproj(tk, x2d, nw, sc, sh, w_in, cos, sin, layer_j):
    n_tab = cos.shape[0] // TM
    tn = 2 * RET_HK
    assert RET_HV == tn and RET_IN == 3 * tn
    vec = pl.BlockSpec((1, D_MODEL), lambda i, j: (0, 0))
    mod = pl.BlockSpec((1, 1, D_MODEL), lambda i, j: (tk.mod_row(i), 0, 0))
    tab = pl.BlockSpec((TM, RET_DK), lambda i, j: (tk.lat_tile(i) % n_tab, 0))
    return pl.pallas_call(
        functools.partial(_inproj_kernel, tc=512, n_ctx=tk.n_ctx),
        out_shape=jax.ShapeDtypeStruct((tk.t, RET_IN), BF16),
        grid=(tk.nt, RET_IN // tn),
        in_specs=[
            pl.BlockSpec((TM, D_MODEL), lambda i, j: (i, 0)),
            vec, mod, mod,
            pl.BlockSpec((None, D_MODEL, tn), lambda i, j: (layer_j, 0, j)),
            tab, tab,
        ],
        out_specs=pl.BlockSpec((TM, tn), lambda i, j: (i, j)),
        scratch_shapes=[pltpu.VMEM((TM, D_MODEL), BF16)],
        compiler_params=_cparams(("parallel", "arbitrary")),
        name="inproj",
    )(x2d, nw, sc, sh, w_in, cos, sin)


def _log_gamma(decay):
    return jnp.log1p(-jnp.exp2(-jnp.full((1, 1), decay, F32)))


def _dot_tn(a, b):
    return lax.dot_general(a, b, (((0,), (0,)), ((), ())), preferred_element_type=F32)


def _dot_nt(a, b):
    return lax.dot_general(a, b, (((1,), (1,)), ((), ())), preferred_element_type=F32)


def _ret_kernel(dec_ref, q_ref, k_ref, v_ref, g_ref, gnw_ref, *rest, heads, has_state):
    if has_state:
        s0_ref, o_ref, oacc_ref, sf_ref, sb_ref = rest
    else:
        (o_ref,) = rest
    c = SLAB
    nchunks = q_ref.shape[1] // c
    row_k = lax.broadcasted_iota(jnp.int32, (c, RET_DK), 0).astype(F32)
    i_f = lax.broadcasted_iota(jnp.int32, (c, c), 0).astype(F32)
    j_f = lax.broadcasted_iota(jnp.int32, (c, c), 1).astype(F32)
    dif = i_f - j_f

    for hh in range(heads):
        head = pl.program_id(1) * heads + hh
        lgf = _log_gamma(dec_ref[0, head])
        lgb = _log_gamma(dec_ref[1, head])
        dmat = jnp.exp(lgf * jnp.maximum(dif, 0.0) + lgb * jnp.maximum(-dif, 0.0))
        gnw = gnw_ref[head]
        ks = slice(hh * RET_DK, (hh + 1) * RET_DK)
        vs = slice(hh * RET_DV, (hh + 1) * RET_DV)

        def intra(r0):
            qc = q_ref[0, pl.ds(r0, c), ks]
            kc = k_ref[0, pl.ds(r0, c), ks]
            vc = v_ref[0, pl.ds(r0, c), vs]
            p = (_dot_nt(qc, kc) * dmat).astype(BF16)
            return qc, kc, vc, jnp.dot(p, vc, preferred_element_type=F32)

        def finalize(r0, o):
            mu = jnp.mean(o, axis=-1, keepdims=True)
            oc = o - mu
            var = jnp.mean(oc * oc, axis=-1, keepdims=True)
            y = oc * lax.rsqrt(var + GN_EPS) * gnw
            hg = 0.5 * g_ref[0, pl.ds(r0, c), vs].astype(F32)
            o_ref[0, pl.ds(r0, c), vs] = ((hg + hg * jnp.tanh(hg)) * y).astype(BF16)

        if not has_state:
            for ci in range(nchunks):
                _, _, _, o = intra(ci * c)
                finalize(ci * c, o)
            continue

        xi_f = jnp.exp(lgf * (row_k + 1.0))
        xi_b = jnp.exp(lgb * (c - row_k))
        zeta_f = jnp.exp(lgf * (c - 1.0 - row_k))
        zeta_b = jnp.exp(lgb * row_k)
        gc_f = jnp.exp(lgf * c)
        gc_b = jnp.exp(lgb * c)
        sf_ref[...] = s0_ref[0, 0, 0, 0]
        sb_ref[...] = s0_ref[0, 0, 1, 0]

        def fwd_part(r0):
            qc, kc, vc, o = intra(r0)
            qx = (qc.astype(F32) * xi_f).astype(BF16)
            o = o + jnp.dot(qx, sf_ref[...].astype(BF16), preferred_element_type=F32)
            kz = (kc.astype(F32) * zeta_f).astype(BF16)
            sf_ref[...] = gc_f * sf_ref[...] + _dot_tn(kz, vc)
            return o

        def bwd_part(r0):
            qc = q_ref[0, pl.ds(r0, c), ks]
            kc = k_ref[0, pl.ds(r0, c), ks]
            vc = v_ref[0, pl.ds(r0, c), vs]
            qx = (qc.astype(F32) * xi_b).astype(BF16)
            o = jnp.dot(qx, sb_ref[...].astype(BF16), preferred_element_type=F32)
            kz = (kc.astype(F32) * zeta_b).astype(BF16)
            sb_ref[...] = gc_b * sb_ref[...] + _dot_tn(kz, vc)
            return o

        def first_half(t, carry):
            rf = pl.multiple_of(t * c, c)
            rb = pl.multiple_of((nchunks - 1 - t) * c, c)
            oacc_ref[pl.ds(rf, c), :] = fwd_part(rf)
            oacc_ref[pl.ds(rb, c), :] = bwd_part(rb)
            return carry

        def second_half(t, carry):
            rf = pl.multiple_of(t * c, c)
            rb = pl.multiple_of((nchunks - 1 - t) * c, c)
            finalize(rf, oacc_ref[pl.ds(rf, c), :] + fwd_part(rf))
            finalize(rb, oacc_ref[pl.ds(rb, c), :] + bwd_part(rb))
            return carry

        assert nchunks % 2 == 0
        lax.fori_loop(0, nchunks // 2, first_half, 0)
        lax.fori_loop(nchunks // 2, nchunks, second_half, 0)


def _retcore(qkvg, b_off, nb, decay, gnw, state, layer_j, heads):
    _, l, _ = qkvg.shape
    has_state = state is not None
    nh = RET_HEADS // heads
    q_off = 0
    k_off = RET_HK // (RET_DK * heads)
    v_off = 2 * RET_HK // (RET_DV * heads)
    g_off = (2 * RET_HK + RET_HV) // (RET_DV * heads)
    in_specs = [
        pl.BlockSpec(memory_space=pltpu.SMEM),
        pl.BlockSpec((1, l, RET_DK * heads), lambda i, h: (b_off + i, 0, q_off + h)),
        pl.BlockSpec((1, l, RET_DK * heads), lambda i, h: (b_off + i, 0, k_off + h)),
        pl.BlockSpec((1, l, RET_DV * heads), lambda i, h: (b_off + i, 0, v_off + h)),
        pl.BlockSpec((1, l, RET_DV * heads), lambda i, h: (b_off + i, 0, g_off + h)),
        pl.BlockSpec((RET_HEADS, 1, RET_DV), lambda i, h: (0, 0, 0)),
    ]
    args = [decay, qkvg, qkvg, qkvg, qkvg, gnw.reshape(RET_HEADS, 1, RET_DV)]
    scratch = []
    if has_state:
        assert heads == 1
        in_specs.append(pl.BlockSpec((1, 1, 2, 1, RET_DK, RET_DV),
                                     lambda i, h: (i, layer_j, 0, h, 0, 0)))
        args.append(state)
        scratch = [pltpu.VMEM((l, RET_DV), F32), pltpu.VMEM((RET_DK, RET_DV), F32),
                   pltpu.VMEM((RET_DK, RET_DV), F32)]
    return pl.pallas_call(
        functools.partial(_ret_kernel, heads=heads, has_state=has_state),
        out_shape=jax.ShapeDtypeStruct((nb, l, RET_HV), BF16),
        grid=(nb, nh),
        in_specs=in_specs,
        out_specs=pl.BlockSpec((1, l, RET_DV * heads), lambda i, h: (i, 0, h)),
        scratch_shapes=scratch,
        compiler_params=_cparams(("parallel", "parallel")),
        name="retcore_state" if has_state else "retcore",
    )(*args)


def _states_kernel(dec_ref, ka_ref, va_ref, kb_ref, vb_ref, o_ref):
    l = ka_ref.shape[1]
    row = lax.broadcasted_iota(jnp.int32, (l, RET_DK), 0).astype(F32)
    for j, (k_ref, v_ref) in enumerate(((ka_ref, va_ref), (kb_ref, vb_ref))):
        for h in range(RET_HEADS):
            lgf = _log_gamma(dec_ref[j, 0, h])
            lgb = _log_gamma(dec_ref[j, 1, h])
            k = k_ref[0, :, h * RET_DK:(h + 1) * RET_DK].astype(F32)
            v = v_ref[0, :, h * RET_DV:(h + 1) * RET_DV]
            kf = (k * jnp.exp(lgf * (l - 1.0 - row))).astype(BF16)
            kb = (k * jnp.exp(lgb * row)).astype(BF16)
            o_ref[0, j, 0, h] = _dot_tn(kf, v)
            o_ref[0, j, 1, h] = _dot_tn(kb, v)


def _states(qkvg_a, qkvg_b, nb, ret_decay):
    _, l, _ = qkvg_a.shape
    kspec = pl.BlockSpec((1, l, RET_HK), lambda i: (i, 0, 1))
    vspec = pl.BlockSpec((1, l, RET_HV), lambda i: (i, 0, 1))
    return pl.pallas_call(
        _states_kernel,
        out_shape=jax.ShapeDtypeStruct((nb, 2, 2, RET_HEADS, RET_DK, RET_DV), F32),
        grid=(nb,),
        in_specs=[pl.BlockSpec(memory_space=pltpu.SMEM), kspec, vspec, kspec, vspec],
        out_specs=pl.BlockSpec((1, 2, 2, RET_HEADS, RET_DK, RET_DV), lambda i: (i, 0, 0, 0, 0, 0)),
        compiler_params=_cparams(("parallel",)),
        name="states",
    )(ret_decay, qkvg_a, qkvg_a, qkvg_b, qkvg_b)


def _outproj_kernel(x_ref, goc_ref, gol_ref, w_ref, g_ref, o_ref, *, n_ctx):
    i = pl.program_id(0)

    def emit(go_ref):
        mix = jnp.dot(go_ref[...], w_ref[...], preferred_element_type=F32)
        o_ref[...] = x_ref[...] + g_ref[0] * mix

    pl.when(i < n_ctx)(lambda: emit(goc_ref))
    pl.when(i >= n_ctx)(lambda: emit(gol_ref))


def _outproj(tk, x2d, go_c, go_l, w_out, ga, layer_j):
    return pl.pallas_call(
        functools.partial(_outproj_kernel, n_ctx=tk.n_ctx),
        out_shape=jax.ShapeDtypeStruct((tk.t, D_MODEL), F32),
        grid=(tk.nt,),
        in_specs=[
            pl.BlockSpec((TM, D_MODEL), lambda i: (i, 0)),
            pl.BlockSpec((TM, RET_HV), lambda i: (tk.ctx_tile(i), 0)),
            pl.BlockSpec((TM, RET_HV), lambda i: (tk.lat_tile(i), 0)),
            pl.BlockSpec((None, RET_HV, D_MODEL), lambda i: (layer_j, 0, 0)),
            pl.BlockSpec((1, 1, D_MODEL), lambda i: (tk.mod_row(i), 0, 0)),
        ],
        out_specs=pl.BlockSpec((TM, D_MODEL), lambda i: (i, 0)),
        compiler_params=_cparams(("arbitrary",)),
        name="outproj",
    )(x2d, go_c, go_l, w_out, ga)


def _rope_tables(length):
    quarter = RET_DK // 4
    half = RET_DK // 2
    rows = length // GRID_W
    freqs = ROPE_BASE ** (-jnp.arange(quarter, dtype=F32) / quarter)
    sign = jnp.concatenate([-jnp.ones((quarter,), F32), jnp.ones((quarter,), F32)])

    def tabs(npos):
        ang = jnp.arange(npos, dtype=F32)[:, None] * freqs[None, :]
        return (jnp.concatenate([jnp.cos(ang), jnp.cos(ang)], axis=-1),
                jnp.concatenate([jnp.sin(ang), jnp.sin(ang)], axis=-1) * sign)

    def by_row(tab):
        return jnp.broadcast_to(tab[:, None, :], (rows, GRID_W, half)).reshape(length, half)

    def by_col(tab):
        return jnp.broadcast_to(tab[None, :, :], (rows, GRID_W, half)).reshape(length, half)

    cos_r, sin_r = tabs(rows)
    cos_c, sin_c = tabs(GRID_W)
    return (jnp.concatenate([by_row(cos_r), by_col(cos_c)], axis=-1),
            jnp.concatenate([by_row(sin_r), by_col(sin_c)], axis=-1))


def kernel(x_prompt, x_sample, state_ret, c, c_ctx, w_ada, b_ada, norm_mix_w, norm_mlp_w, pool_w,
           pool_b, pool_scale, ret_w_in, ret_decay, ret_gn_w, ret_w_out, mlp_w1, mlp_w2,
           final_norm_w):
    nb_ctx, seq, _ = x_prompt.shape
    nb_lat, lat, _ = x_sample.shape
    tk = _Tokens(nb_ctx * seq, lat, nb_lat)
    assert tk.nc == lat and seq == SLAB

    cond8 = jnp.concatenate([c_ctx[None, :], c, jnp.zeros((8 - 1 - nb_lat, D_MODEL), F32)], axis=0)
    mods = _mods(cond8, w_ada, b_ada)
    mods = mods.reshape(DEPTH, 8, N_MOD, D_MODEL).transpose(0, 2, 1, 3)
    mods = mods[:, :, :1 + nb_lat, None, :]

    cos, sin = _rope_tables(lat)
    inv = jnp.stack([_inv_count_table(lat, False), _inv_count_table(lat, True)])
    w_in, w_out = ret_w_in.astype(BF16), ret_w_out.astype(BF16)
    w1, w2, pw = mlp_w1.astype(BF16), mlp_w2.astype(BF16), pool_w.astype(BF16)
    fw = final_norm_w.reshape(1, D_MODEL)

    x, rsb = _pack(tk, x_prompt.reshape(tk.nc, D_MODEL), x_sample.reshape(nb_lat * lat, D_MODEL))
    qkvgs = []
    for i in range(DEPTH):
        j = i // 2
        sh_a, sc_a, g_a, sh_m, sc_m, g_m = (mods[i, k] for k in range(N_MOD))
        nw_a = norm_mix_w[i].reshape(1, D_MODEL)
        if i % 2 == 0:
            x = _pool(x.reshape(1 + nb_lat, lat, D_MODEL), rsb.reshape(1 + nb_lat, lat, LANES), inv,
                      nw_a, sc_a, sh_a, g_a, pool_scale[j].reshape(1, D_MODEL),
                      pool_b[j].reshape(1, D_MODEL), pw, j).reshape(tk.t, D_MODEL)
        else:
            qkvg = _inproj(tk, x, nw_a, sc_a, sh_a, w_in, cos, sin, j)
            qkvgs.append(qkvg)
            go_c = _retcore(qkvg.reshape(tk.t // seq, seq, RET_IN), 0, nb_ctx, ret_decay[j],
                            ret_gn_w[j], None, j, RET_HEADS)
            go_l = _retcore(qkvg.reshape(tk.t // lat, lat, RET_IN), tk.nc // lat, nb_lat,
                            ret_decay[j], ret_gn_w[j], state_ret, j, 1)
            x = _outproj(tk, x, go_c.reshape(tk.nc, RET_HV), go_l.reshape(nb_lat * lat, RET_HV),
                         w_out, g_a, j)
        x, rsb = _mlp(tk, x, norm_mlp_w[i].reshape(1, D_MODEL), sc_m, sh_m, g_m, w1, w2, fw, i, 2048,
                      i == DEPTH - 1)
    y_prompt, y_sample = x, rsb
    new_state = _states(qkvgs[0].reshape(tk.t // seq, seq, RET_IN),
                        qkvgs[1].reshape(tk.t // seq, seq, RET_IN), nb_ctx, ret_decay)
    return (y_prompt.reshape(x_prompt.shape), y_sample.reshape(x_sample.shape), new_state)
```

```python
import functools

import jax
import jax.numpy as jnp
from jax import lax
from jax.experimental import pallas as pl
from jax.experimental.pallas import tpu as pltpu

F32 = jnp.float32
BF16 = jnp.bfloat16

D_MODEL = 1024
DEPTH = 4
GRID_W = 64
POOL_WINDOWS = (2, 4, 8, 16)
POOL_GC = 256
RET_HEADS = 4
RET_DK = 256
RET_DV = 512
RET_HK = RET_HEADS * RET_DK
RET_HV = RET_HEADS * RET_DV
RET_IN = 2 * RET_HK + 2 * RET_HV
D_FF = 4 * D_MODEL
ROPE_BASE = 10000.0
NORM_EPS = 1e-6
GN_EPS = 1e-5
N_MOD = 6

LANES = 128
SLAB = 256
TM = 1024
TM_MLP = 1024
VMEM_LIMIT = 56 * 1024 * 1024


def _cparams(sem):
    return pltpu.CompilerParams(dimension_semantics=sem, vmem_limit_bytes=VMEM_LIMIT)


def _norm_mod(x, nw, sc, sh):
    ms = jnp.mean(x * x, axis=-1, keepdims=True)
    return (x * lax.rsqrt(ms + NORM_EPS)) * nw * (1.0 + sc) + sh


def _rsqrt_ms_lanes(x):
    rs = lax.rsqrt(jnp.mean(x * x, axis=-1, keepdims=True) + NORM_EPS)
    return jnp.broadcast_to(rs, (x.shape[0], LANES))


class _Tokens:
    def __init__(self, nc, lat, nlat, tm):
        assert nc % tm == 0 and lat % tm == 0
        self.nc, self.lat, self.nlat, self.tm = nc, lat, nlat, tm
        self.t = nc + lat * nlat
        self.n_ctx = nc // tm
        self.nt = self.t // tm

    def mod_row(self, i):
        return jnp.where(i < self.n_ctx, 0, 1 + (i - self.n_ctx) // (self.lat // self.tm))

    def ctx_tile(self, i):
        return jnp.minimum(i, self.n_ctx - 1)

    def lat_tile(self, i):
        return jnp.maximum(i - self.n_ctx, 0)


class _Cast:
    def __init__(self, w, layer, nblk):
        _, r, c = w.shape
        assert r % (16 * nblk) == 0
        self.w, self.layer, self.nblk, self.rows, self.cols = w, layer, nblk, r // nblk, c
        self.out_shape = jax.ShapeDtypeStruct((r, c), BF16)

    def specs(self, step):
        def blk(*ids):
            return jnp.minimum(step(*ids), self.nblk - 1)
        return (pl.BlockSpec((None, self.rows, self.cols), lambda *ids: (self.layer, blk(*ids), 0)),
                pl.BlockSpec((self.rows, self.cols), lambda *ids: (blk(*ids), 0)))


def _do_casts(src_refs, dst_refs, nblks, step):
    for src_ref, dst_ref, nblk in zip(src_refs, dst_refs, nblks):
        @pl.when(step < nblk)
        def _(src_ref=src_ref, dst_ref=dst_ref):
            dst_ref[...] = src_ref[...].astype(BF16)


def _mods_kernel(c_ref, w_ref, b_ref, o_ref):
    c = c_ref[...]
    s = c * jax.nn.sigmoid(c)
    w = w_ref[0].astype(BF16)
    o_ref[0] = jnp.dot(s.astype(BF16), w, preferred_element_type=F32) + b_ref[0]


def _mods(cond8, w_ada, b_ada):
    tn = 1024
    n = N_MOD * D_MODEL
    return pl.pallas_call(
        _mods_kernel,
        out_shape=jax.ShapeDtypeStruct((DEPTH, 8, n), F32),
        grid=(DEPTH, n // tn),
        in_specs=[
            pl.BlockSpec((8, D_MODEL), lambda l, j: (0, 0)),
            pl.BlockSpec((1, D_MODEL, tn), lambda l, j: (l, 0, j)),
            pl.BlockSpec((1, 1, tn), lambda l, j: (l, 0, j)),
        ],
        out_specs=pl.BlockSpec((1, 8, tn), lambda l, j: (l, 0, j)),
        compiler_params=_cparams(("parallel", "parallel")),
        name="mods",
    )(cond8, w_ada, b_ada.reshape(DEPTH, 1, n))


def _pack_kernel(xc_ref, xl_ref, o_ref, rs_ref, *, n_ctx):
    i = pl.program_id(0)

    def emit(x_ref):
        x = x_ref[...]
        o_ref[...] = x
        rs_ref[...] = _rsqrt_ms_lanes(x)

    pl.when(i < n_ctx)(lambda: emit(xc_ref))
    pl.when(i >= n_ctx)(lambda: emit(xl_ref))


def _pack(tk, xc, xl):
    return pl.pallas_call(
        functools.partial(_pack_kernel, n_ctx=tk.n_ctx),
        out_shape=(jax.ShapeDtypeStruct((tk.t, D_MODEL), F32), jax.ShapeDtypeStruct((tk.t, LANES), F32)),
        grid=(tk.nt,),
        in_specs=[pl.BlockSpec((tk.tm, D_MODEL), lambda i: (tk.ctx_tile(i), 0)),
                  pl.BlockSpec((tk.tm, D_MODEL), lambda i: (tk.lat_tile(i), 0))],
        out_specs=(pl.BlockSpec((tk.tm, D_MODEL), lambda i: (i, 0)),
                   pl.BlockSpec((tk.tm, LANES), lambda i: (i, 0))),
        compiler_params=_cparams(("arbitrary",)),
        name="pack",
    )(xc, xl)


def _window_count(pos, w, length):
    half = w // 2
    return jnp.minimum(pos - half + w, length) - jnp.maximum(pos - half, 0)


def _inv_count_table(lb, grid):
    t = jnp.arange(lb)
    rows = []
    for w in POOL_WINDOWS:
        if grid:
            cnt = _window_count(t // GRID_W, w, lb // GRID_W) * _window_count(t % GRID_W, w, GRID_W)
        else:
            cnt = _window_count(t % SLAB, w, SLAB)
        rows.append(1.0 / cnt.astype(F32))
    return jnp.broadcast_to(jnp.stack(rows)[:, :, None], (len(POOL_WINDOWS), lb, LANES))


def _pool_group(w, grid, x_ref, rs_ref, inv_ref, nw_ref, sc_ref, sh_ref, ga_ref, ps_ref, pb_ref,
                pw_ref, o_ref, colp_ref, hbuf_ref):
    half = w // 2
    nslab = x_ref.shape[1] // SLAB
    t_i = lax.broadcasted_iota(jnp.int32, (SLAB, SLAB), 0)
    s_i = lax.broadcasted_iota(jnp.int32, (SLAB, SLAB), 1)
    diff = s_i - t_i
    band = (diff >= -half) & (diff <= w - half - 1)
    if grid:
        band = band & ((s_i >> 6) == (t_i >> 6))
    sm = jnp.where(band, 1.0, 0.0).astype(BF16)

    def inv_cnt(r0):
        v = inv_ref[pl.ds(r0, SLAB), :]
        return jnp.concatenate([v, v], axis=1)

    a = nw_ref[...] * (1.0 + sc_ref[0])
    sh = sh_ref[0]
    scale_out = ga_ref[0] * ps_ref[...]
    pw = pw_ref[...]
    pb = pb_ref[...]

    def slab_h(r0):
        x = x_ref[0, pl.ds(r0, SLAB), :]
        rs = rs_ref[0, pl.ds(r0, SLAB), :]
        rs2 = jnp.concatenate([rs, rs], axis=1)
        return x, x * rs2 * a + sh

    def window_sum(h):
        hi = h.astype(BF16)
        lo = (h - hi.astype(F32)).astype(BF16)
        return (jnp.dot(sm, hi, preferred_element_type=F32)
                + jnp.dot(sm, lo, preferred_element_type=F32))

    def finish(r0, x, h, m):
        d = (m - h).astype(BF16)
        mix = jnp.dot(d, pw, preferred_element_type=F32) + pb
        o_ref[0, pl.ds(r0, SLAB), :] = x + scale_out * mix

    if not grid:
        def seq_body(s, carry):
            r0 = pl.multiple_of(s * SLAB, SLAB)
            x, h = slab_h(r0)
            finish(r0, x, h, window_sum(h) * inv_cnt(r0))
            return carry

        lax.fori_loop(0, nslab, seq_body, 0, unroll=2)
        return

    pad = 8 * GRID_W
    zeros = jnp.zeros((pad, POOL_GC), F32)
    colp_ref[pl.ds(0, pad), :] = zeros
    colp_ref[pl.ds(pad + nslab * SLAB, pad), :] = zeros

    def col_body(s, carry):
        r0 = pl.multiple_of(s * SLAB, SLAB)
        _, h = slab_h(r0)
        hbuf_ref[pl.ds(r0, SLAB), :] = h
        colp_ref[pl.ds(pad + r0, SLAB), :] = window_sum(h)
        return carry

    lax.fori_loop(0, nslab, col_body, 0, unroll=2)

    def row_body(s, carry):
        r0 = pl.multiple_of(s * SLAB, SLAB)
        acc = colp_ref[pl.ds(pad + r0 - half * GRID_W, SLAB), :]
        for j in range(1, w):
            acc = acc + colp_ref[pl.ds(pad + r0 + (j - half) * GRID_W, SLAB), :]
        finish(r0, x_ref[0, pl.ds(r0, SLAB), :], hbuf_ref[pl.ds(r0, SLAB), :], acc * inv_cnt(r0))
        return carry

    lax.fori_loop(0, nslab, row_body, 0, unroll=2)


def _pool_kernel(x_ref, rs_ref, inv_ref, nw_ref, sc_ref, sh_ref, ga_ref, ps_ref, pb_ref, pw_ref, *rest,
                 cast_nblk):
    n_cast = len(cast_nblk)
    cast_in, o_ref, cast_out = rest[:n_cast], rest[n_cast], rest[n_cast + 1:2 * n_cast + 1]
    colp_ref, hbuf_ref = rest[2 * n_cast + 1:]
    b = pl.program_id(0)
    g = pl.program_id(1)
    _do_casts(cast_in, cast_out, cast_nblk, b * pl.num_programs(1) + g)
    for gi, w in enumerate(POOL_WINDOWS):
        for grid in (False, True):
            @pl.when((g == gi) & ((b > 0) if grid else (b == 0)))
            def _(w=w, grid=grid):
                _pool_group(w, grid, x_ref, rs_ref, inv_ref, nw_ref, sc_ref, sh_ref, ga_ref, ps_ref,
                            pb_ref, pw_ref, o_ref, colp_ref, hbuf_ref)


def _pool(x3, rsb3, inv, nw, sc, sh, ga, ps, pb, pw_bf16, layer_j, casts=()):
    nb, lb, _ = x3.shape
    ng = len(POOL_WINDOWS)
    cast_specs = [cs.specs(lambda b, g: b * ng + g) for cs in casts]
    vec = pl.BlockSpec((1, POOL_GC), lambda b, g: (0, g))
    mod = pl.BlockSpec((1, 1, POOL_GC), lambda b, g: (b, 0, g))
    return pl.pallas_call(
        functools.partial(_pool_kernel, cast_nblk=tuple(cs.nblk for cs in casts)),
        out_shape=(jax.ShapeDtypeStruct(x3.shape, F32),) + tuple(cs.out_shape for cs in casts),
        grid=(nb, ng),
        in_specs=[
            pl.BlockSpec((1, lb, POOL_GC), lambda b, g: (b, 0, g)),
            pl.BlockSpec((1, lb, LANES), lambda b, g: (b, 0, 0)),
            pl.BlockSpec((None, None, lb, LANES), lambda b, g: (jnp.minimum(b, 1), g, 0, 0)),
            vec, mod, mod, mod, vec, vec,
            pl.BlockSpec((None, None, POOL_GC, POOL_GC), lambda b, g: (layer_j, g, 0, 0)),
        ] + [s[0] for s in cast_specs],
        out_specs=(pl.BlockSpec((1, lb, POOL_GC), lambda b, g: (b, 0, g)),)
        + tuple(s[1] for s in cast_specs),
        scratch_shapes=[pltpu.VMEM((lb + 16 * GRID_W, POOL_GC), F32), pltpu.VMEM((lb, POOL_GC), F32)],
        compiler_params=_cparams(("arbitrary", "arbitrary")),
        name="pool",
    )(x3, rsb3, inv, nw, sc, sh, ga, ps, pb, pw_bf16, *[cs.w for cs in casts])


def _mlp_kernel(x_ref, nw_ref, sc_ref, sh_ref, g_ref, w1_ref, w2_ref, fw_ref, *rest, final, fc, n_ctx,
                cast_nblk):
    n_cast = len(cast_nblk)
    cast_in, rest = rest[:n_cast], rest[n_cast:]
    outs, (h_ref, acc_ref) = rest[:2 + n_cast], rest[2 + n_cast:]
    if final:
        yc_ref, yl_ref = outs[:2]
    else:
        o_ref, rs_ref = outs[:2]
    i = pl.program_id(0)
    j = pl.program_id(1)
    nj = pl.num_programs(1)

    _do_casts(cast_in, outs[2:], cast_nblk, i * nj + j)

    def ffn(h):
        out = None
        for c0 in range(0, w1_ref.shape[1], fc):
            a = jnp.dot(h, w1_ref[:, c0:c0 + fc], preferred_element_type=F32)
            a = jnp.maximum(a, 0.0)
            d = jnp.dot((a * a).astype(BF16), w2_ref[c0:c0 + fc, :], preferred_element_type=F32)
            out = d if out is None else out + d
        return out

    @pl.when(j == 0)
    def _():
        h = _norm_mod(x_ref[...], nw_ref[...], sc_ref[0], sh_ref[0]).astype(BF16)
        h_ref[...] = h
        acc_ref[...] = ffn(h)

    @pl.when((j > 0) & (j < nj - 1))
    def _():
        acc_ref[...] += ffn(h_ref[...])

    @pl.when(j == nj - 1)
    def _():
        y = x_ref[...] + g_ref[0] * (acc_ref[...] + ffn(h_ref[...]))
        if not final:
            o_ref[...] = y
            rs_ref[...] = _rsqrt_ms_lanes(y)
            return
        rs = lax.rsqrt(jnp.mean(y * y, axis=-1, keepdims=True) + NORM_EPS)
        y = (y * rs) * fw_ref[...]

        @pl.when(i < n_ctx)
        def _():
            yc_ref[...] = y

        @pl.when(i >= n_ctx)
        def _():
            yl_ref[...] = y


def _mlp(tk, x2d, nw, sc, sh, g, w1, w2, fw, tf, final, casts=()):
    nj = D_FF // tf
    assert nj >= 2
    cast_specs = [cs.specs(lambda i, j: i * nj + j) for cs in casts]
    vec = pl.BlockSpec((1, D_MODEL), lambda i, j: (0, 0))
    mod = pl.BlockSpec((1, 1, D_MODEL), lambda i, j: (tk.mod_row(i), 0, 0))
    if final:
        out_shape = (jax.ShapeDtypeStruct((tk.nc, D_MODEL), F32),
                     jax.ShapeDtypeStruct((tk.t - tk.nc, D_MODEL), F32))
        out_specs = (pl.BlockSpec((tk.tm, D_MODEL), lambda i, j: (tk.ctx_tile(i), 0)),
                     pl.BlockSpec((tk.tm, D_MODEL), lambda i, j: (tk.lat_tile(i), 0)))
    else:
        out_shape = (jax.ShapeDtypeStruct((tk.t, D_MODEL), F32), jax.ShapeDtypeStruct((tk.t, LANES), F32))
        out_specs = (pl.BlockSpec((tk.tm, D_MODEL), lambda i, j: (i, 0)),
                     pl.BlockSpec((tk.tm, LANES), lambda i, j: (i, 0)))
    return pl.pallas_call(
        functools.partial(_mlp_kernel, final=final, fc=min(tf, 1024), n_ctx=tk.n_ctx,
                          cast_nblk=tuple(cs.nblk for cs in casts)),
        out_shape=out_shape + tuple(cs.out_shape for cs in casts),
        grid=(tk.nt, nj),
        in_specs=[
            pl.BlockSpec((tk.tm, D_MODEL), lambda i, j: (i, 0)),
            vec, mod, mod, mod,
            pl.BlockSpec((D_MODEL, tf), lambda i, j: (0, j)),
            pl.BlockSpec((tf, D_MODEL), lambda i, j: (j, 0)),
            vec,
        ] + [s[0] for s in cast_specs],
        out_specs=out_specs + tuple(s[1] for s in cast_specs),
        scratch_shapes=[pltpu.VMEM((tk.tm, D_MODEL), BF16), pltpu.VMEM((tk.tm, D_MODEL), F32)],
        compiler_params=_cparams(("arbitrary", "arbitrary")),
        name="mlp",
    )(x2d, nw, sc, sh, g, w1, w2, fw, *[cs.w for cs in casts])


def _inproj_kernel(x_ref, nw_ref, sc_ref, sh_ref, w_ref, cos_ref, sin_ref, *rest, tc, n_ctx, cast_nblk):
    n_cast = len(cast_nblk)
    cast_in, o_ref, cast_out, h_ref = (rest[:n_cast], rest[n_cast], rest[n_cast + 1:2 * n_cast + 1],
                                       rest[2 * n_cast + 1])
    i = pl.program_id(0)
    j = pl.program_id(1)
    nchunk = o_ref.shape[1] // tc
    _do_casts(cast_in, cast_out, cast_nblk, i * pl.num_programs(1) + j)

    def qk_step(rope):
        h = _norm_mod(x_ref[...], nw_ref[...], sc_ref[0], sh_ref[0]).astype(BF16)
        h_ref[...] = h
        for c in range(nchunk):
            p = jnp.dot(h, w_ref[:, c * tc:(c + 1) * tc], preferred_element_type=F32)
            if c * tc >= RET_HK:
                p = p * (RET_DK ** -0.5)
            if not rope:
                o_ref[:, c * tc:(c + 1) * tc] = p.astype(BF16)
                continue
            for t in range(tc // LANES):
                pt = p[:, t * LANES:(t + 1) * LANES]
                c0 = (t % 2) * LANES
                rot = pltpu.roll(pt, LANES // 2, axis=1)
                pt = pt * cos_ref[:, c0:c0 + LANES] + rot * sin_ref[:, c0:c0 + LANES]
                o_ref[:, c * tc + t * LANES:c * tc + (t + 1) * LANES] = pt.astype(BF16)

    pl.when((j == 0) & (i < n_ctx))(lambda: qk_step(False))
    pl.when((j == 0) & (i >= n_ctx))(lambda: qk_step(True))

    @pl.when(j > 0)
    def _():
        h = h_ref[...]
        for c in range(nchunk):
            p = jnp.dot(h, w_ref[:, c * tc:(c + 1) * tc], preferred_element_type=F32)
            o_ref[:, c * tc:(c + 1) * tc] = p.astype(BF16)


def _inproj(tk, x2d, nw, sc, sh, w_in, cos, sin, casts=()):
    n_tab = cos.shape[0] // tk.tm
    tn = 2 * RET_HK
    nj = RET_IN // tn
    assert RET_HV == tn and RET_IN == 3 * tn
    cast_specs = [cs.specs(lambda i, j: i * nj + j) for cs in casts]
    vec = pl.BlockSpec((1, D_MODEL), lambda i, j: (0, 0))
    mod = pl.BlockSpec((1, 1, D_MODEL), lambda i, j: (tk.mod_row(i), 0, 0))
    tab = pl.BlockSpec((tk.tm, RET_DK), lambda i, j: (tk.lat_tile(i) % n_tab, 0))
    return pl.pallas_call(
        functools.partial(_inproj_kernel, tc=512, n_ctx=tk.n_ctx,
                          cast_nblk=tuple(cs.nblk for cs in casts)),
        out_shape=(jax.ShapeDtypeStruct((tk.t, RET_IN), BF16),) + tuple(cs.out_shape for cs in casts),
        grid=(tk.nt, nj),
        in_specs=[
            pl.BlockSpec((tk.tm, D_MODEL), lambda i, j: (i, 0)),
            vec, mod, mod,
            pl.BlockSpec((D_MODEL, tn), lambda i, j: (0, j)),
            tab, tab,
        ] + [s[0] for s in cast_specs],
        out_specs=(pl.BlockSpec((tk.tm, tn), lambda i, j: (i, j)),) + tuple(s[1] for s in cast_specs),
        scratch_shapes=[pltpu.VMEM((tk.tm, D_MODEL), BF16)],
        compiler_params=_cparams(("arbitrary", "arbitrary")),
        name="inproj",
    )(x2d, nw, sc, sh, w_in, cos, sin, *[cs.w for cs in casts])


def _log_gamma(decay):
    return jnp.log1p(-jnp.exp2(-jnp.full((1, 1), decay, F32)))


def _dot_tn(a, b):
    return lax.dot_general(a, b, (((0,), (0,)), ((), ())), preferred_element_type=F32)


def _dot_nt(a, b):
    return lax.dot_general(a, b, (((1,), (1,)), ((), ())), preferred_element_type=F32)


def _ret_kernel(dec_ref, q_ref, k_ref, v_ref, g_ref, gnw_ref, *rest, heads, has_state, cast_nblk):
    n_cast = len(cast_nblk)
    if has_state:
        s0_ref, rest = rest[0], rest[1:]
    cast_in, o_ref, cast_out = rest[:n_cast], rest[n_cast], rest[n_cast + 1:2 * n_cast + 1]
    if has_state:
        oacc_ref, sf_ref, sb_ref = rest[2 * n_cast + 1:]
    _do_casts(cast_in, cast_out, cast_nblk, pl.program_id(0) * pl.num_programs(1) + pl.program_id(1))
    c = SLAB
    nchunks = q_ref.shape[1] // c
    row_k = lax.broadcasted_iota(jnp.int32, (c, RET_DK), 0).astype(F32)
    i_f = lax.broadcasted_iota(jnp.int32, (c, c), 0).astype(F32)
    j_f = lax.broadcasted_iota(jnp.int32, (c, c), 1).astype(F32)
    dif = i_f - j_f

    for hh in range(heads):
        head = pl.program_id(1) * heads + hh
        lgf = _log_gamma(dec_ref[0, head])
        lgb = _log_gamma(dec_ref[1, head])
        dmat = jnp.exp(lgf * jnp.maximum(dif, 0.0) + lgb * jnp.maximum(-dif, 0.0))
        gnw = gnw_ref[head]
        ks = slice(hh * RET_DK, (hh + 1) * RET_DK)
        vs = slice(hh * RET_DV, (hh + 1) * RET_DV)

        def intra(r0):
            qc = q_ref[0, pl.ds(r0, c), ks]
            kc = k_ref[0, pl.ds(r0, c), ks]
            vc = v_ref[0, pl.ds(r0, c), vs]
            p = (_dot_nt(qc, kc) * dmat).astype(BF16)
            return qc, kc, vc, jnp.dot(p, vc, preferred_element_type=F32)

        def finalize(r0, o):
            mu = jnp.mean(o, axis=-1, keepdims=True)
            oc = o - mu
            var = jnp.mean(oc * oc, axis=-1, keepdims=True)
            y = oc * lax.rsqrt(var + GN_EPS) * gnw
            hg = 0.5 * g_ref[0, pl.ds(r0, c), vs].astype(F32)
            o_ref[0, pl.ds(r0, c), vs] = ((hg + hg * jnp.tanh(hg)) * y).astype(BF16)

        if not has_state:
            for ci in range(nchunks):
                _, _, _, o = intra(ci * c)
                finalize(ci * c, o)
            continue

        xi_f = jnp.exp(lgf * (row_k + 1.0))
        xi_b = jnp.exp(lgb * (c - row_k))
        zeta_f = jnp.exp(lgf * (c - 1.0 - row_k))
        zeta_b = jnp.exp(lgb * row_k)
        gc_f = jnp.exp(lgf * c)
        gc_b = jnp.exp(lgb * c)
        sf_ref[...] = s0_ref[0, 0, 0, 0]
        sb_ref[...] = s0_ref[0, 0, 1, 0]

        def fwd_part(r0):
            qc, kc, vc, o = intra(r0)
            qx = (qc.astype(F32) * xi_f).astype(BF16)
            o = o + jnp.dot(qx, sf_ref[...].astype(BF16), preferred_element_type=F32)
            kz = (kc.astype(F32) * zeta_f).astype(BF16)
            sf_ref[...] = gc_f * sf_ref[...] + _dot_tn(kz, vc)
            return o

        def bwd_part(r0):
            qc = q_ref[0, pl.ds(r0, c), ks]
            kc = k_ref[0, pl.ds(r0, c), ks]
            vc = v_ref[0, pl.ds(r0, c), vs]
            qx = (qc.astype(F32) * xi_b).astype(BF16)
            o = jnp.dot(qx, sb_ref[...].astype(BF16), preferred_element_type=F32)
            kz = (kc.astype(F32) * zeta_b).astype(BF16)
            sb_ref[...] = gc_b * sb_ref[...] + _dot_tn(kz, vc)
            return o

        def first_half(t, carry):
            rf = pl.multiple_of(t * c, c)
            rb = pl.multiple_of((nchunks - 1 - t) * c, c)
            oacc_ref[pl.ds(rf, c), :] = fwd_part(rf)
            oacc_ref[pl.ds(rb, c), :] = bwd_part(rb)
            return carry

        def second_half(t, carry):
            rf = pl.multiple_of(t * c, c)
            rb = pl.multiple_of((nchunks - 1 - t) * c, c)
            finalize(rf, oacc_ref[pl.ds(rf, c), :] + fwd_part(rf))
            finalize(rb, oacc_ref[pl.ds(rb, c), :] + bwd_part(rb))
            return carry

        assert nchunks % 2 == 0
        lax.fori_loop(0, nchunks // 2, first_half, 0)
        lax.fori_loop(nchunks // 2, nchunks, second_half, 0)


def _retcore(qkvg, b_off, nb, decay, gnw, state, layer_j, heads, casts=()):
    _, l, _ = qkvg.shape
    has_state = state is not None
    nh = RET_HEADS // heads
    q_off = 0
    k_off = RET_HK // (RET_DK * heads)
    v_off = 2 * RET_HK // (RET_DV * heads)
    g_off = (2 * RET_HK + RET_HV) // (RET_DV * heads)
    in_specs = [
        pl.BlockSpec(memory_space=pltpu.SMEM),
        pl.BlockSpec((1, l, RET_DK * heads), lambda i, h: (b_off + i, 0, q_off + h)),
        pl.BlockSpec((1, l, RET_DK * heads), lambda i, h: (b_off + i, 0, k_off + h)),
        pl.BlockSpec((1, l, RET_DV * heads), lambda i, h: (b_off + i, 0, v_off + h)),
        pl.BlockSpec((1, l, RET_DV * heads), lambda i, h: (b_off + i, 0, g_off + h)),
        pl.BlockSpec((RET_HEADS, 1, RET_DV), lambda i, h: (0, 0, 0)),
    ]
    args = [decay, qkvg, qkvg, qkvg, qkvg, gnw.reshape(RET_HEADS, 1, RET_DV)]
    scratch = []
    if has_state:
        assert heads == 1
        in_specs.append(pl.BlockSpec((1, 1, 2, 1, RET_DK, RET_DV),
                                     lambda i, h: (i, layer_j, 0, h, 0, 0)))
        args.append(state)
        scratch = [pltpu.VMEM((l, RET_DV), F32), pltpu.VMEM((RET_DK, RET_DV), F32),
                   pltpu.VMEM((RET_DK, RET_DV), F32)]
    cast_specs = [cs.specs(lambda i, h: i * nh + h) for cs in casts]
    return pl.pallas_call(
        functools.partial(_ret_kernel, heads=heads, has_state=has_state,
                          cast_nblk=tuple(cs.nblk for cs in casts)),
        out_shape=(jax.ShapeDtypeStruct((nb, l, RET_HV), BF16),) + tuple(cs.out_shape for cs in casts),
        grid=(nb, nh),
        in_specs=in_specs + [s[0] for s in cast_specs],
        out_specs=(pl.BlockSpec((1, l, RET_DV * heads), lambda i, h: (i, 0, h)),)
        + tuple(s[1] for s in cast_specs),
        scratch_shapes=scratch,
        compiler_params=_cparams(("arbitrary", "arbitrary")),
        name="retcore_state" if has_state else "retcore",
    )(*args, *[cs.w for cs in casts])


def _states_kernel(dec_ref, ka_ref, va_ref, kb_ref, vb_ref, o_ref):
    l = ka_ref.shape[1]
    row = lax.broadcasted_iota(jnp.int32, (l, RET_DK), 0).astype(F32)
    for j, (k_ref, v_ref) in enumerate(((ka_ref, va_ref), (kb_ref, vb_ref))):
        for h in range(RET_HEADS):
            lgf = _log_gamma(dec_ref[j, 0, h])
            lgb = _log_gamma(dec_ref[j, 1, h])
            k = k_ref[0, :, h * RET_DK:(h + 1) * RET_DK].astype(F32)
            v = v_ref[0, :, h * RET_DV:(h + 1) * RET_DV]
            kf = (k * jnp.exp(lgf * (l - 1.0 - row))).astype(BF16)
            kb = (k * jnp.exp(lgb * row)).astype(BF16)
            o_ref[0, j, 0, h] = _dot_tn(kf, v)
            o_ref[0, j, 1, h] = _dot_tn(kb, v)


def _states(qkvg_a, qkvg_b, nb, ret_decay):
    _, l, _ = qkvg_a.shape
    kspec = pl.BlockSpec((1, l, RET_HK), lambda i: (i, 0, 1))
    vspec = pl.BlockSpec((1, l, RET_HV), lambda i: (i, 0, 1))
    return pl.pallas_call(
        _states_kernel,
        out_shape=jax.ShapeDtypeStruct((nb, 2, 2, RET_HEADS, RET_DK, RET_DV), F32),
        grid=(nb,),
        in_specs=[pl.BlockSpec(memory_space=pltpu.SMEM), kspec, vspec, kspec, vspec],
        out_specs=pl.BlockSpec((1, 2, 2, RET_HEADS, RET_DK, RET_DV), lambda i: (i, 0, 0, 0, 0, 0)),
        compiler_params=_cparams(("parallel",)),
        name="states",
    )(ret_decay, qkvg_a, qkvg_a, qkvg_b, qkvg_b)


def _outproj_kernel(x_ref, goc_ref, gol_ref, w_ref, g_ref, o_ref, *, n_ctx):
    i = pl.program_id(0)

    def emit(go_ref):
        mix = jnp.dot(go_ref[...], w_ref[...], preferred_element_type=F32)
        o_ref[...] = x_ref[...] + g_ref[0] * mix

    pl.when(i < n_ctx)(lambda: emit(goc_ref))
    pl.when(i >= n_ctx)(lambda: emit(gol_ref))


def _outproj(tk, x2d, go_c, go_l, w_out, ga):
    return pl.pallas_call(
        functools.partial(_outproj_kernel, n_ctx=tk.n_ctx),
        out_shape=jax.ShapeDtypeStruct((tk.t, D_MODEL), F32),
        grid=(tk.nt,),
        in_specs=[
            pl.BlockSpec((tk.tm, D_MODEL), lambda i: (i, 0)),
            pl.BlockSpec((tk.tm, RET_HV), lambda i: (tk.ctx_tile(i), 0)),
            pl.BlockSpec((tk.tm, RET_HV), lambda i: (tk.lat_tile(i), 0)),
            pl.BlockSpec((RET_HV, D_MODEL), lambda i: (0, 0)),
            pl.BlockSpec((1, 1, D_MODEL), lambda i: (tk.mod_row(i), 0, 0)),
        ],
        out_specs=pl.BlockSpec((tk.tm, D_MODEL), lambda i: (i, 0)),
        compiler_params=_cparams(("arbitrary",)),
        name="outproj",
    )(x2d, go_c, go_l, w_out, ga)


def _rope_tables(length):
    quarter = RET_DK // 4
    half = RET_DK // 2
    rows = length // GRID_W
    freqs = ROPE_BASE ** (-jnp.arange(quarter, dtype=F32) / quarter)
    sign = jnp.concatenate([-jnp.ones((quarter,), F32), jnp.ones((quarter,), F32)])

    def tabs(npos):
        ang = jnp.arange(npos, dtype=F32)[:, None] * freqs[None, :]
        return (jnp.concatenate([jnp.cos(ang), jnp.cos(ang)], axis=-1),
                jnp.concatenate([jnp.sin(ang), jnp.sin(ang)], axis=-1) * sign)

    def by_row(tab):
        return jnp.broadcast_to(tab[:, None, :], (rows, GRID_W, half)).reshape(length, half)

    def by_col(tab):
        return jnp.broadcast_to(tab[None, :, :], (rows, GRID_W, half)).reshape(length, half)

    cos_r, sin_r = tabs(rows)
    cos_c, sin_c = tabs(GRID_W)
    return (jnp.concatenate([by_row(cos_r), by_col(cos_c)], axis=-1),
            jnp.concatenate([by_row(sin_r), by_col(sin_c)], axis=-1))


def kernel(x_prompt, x_sample, state_ret, c, c_ctx, w_ada, b_ada, norm_mix_w, norm_mlp_w, pool_w,
           pool_b, pool_scale, ret_w_in, ret_decay, ret_gn_w, ret_w_out, mlp_w1, mlp_w2,
           final_norm_w):
    nb_ctx, seq, _ = x_prompt.shape
    nb_lat, lat, _ = x_sample.shape
    tk = _Tokens(nb_ctx * seq, lat, nb_lat, TM)
    tk_mlp = _Tokens(nb_ctx * seq, lat, nb_lat, TM_MLP)
    assert tk.nc == lat and seq == SLAB

    cond8 = jnp.concatenate([c_ctx[None, :], c, jnp.zeros((8 - 1 - nb_lat, D_MODEL), F32)], axis=0)
    mods = _mods(cond8, w_ada, b_ada)
    mods = mods.reshape(DEPTH, 8, N_MOD, D_MODEL).transpose(0, 2, 1, 3)
    mods = mods[:, :, :1 + nb_lat, None, :]

    cos, sin = _rope_tables(lat)
    inv = jnp.stack([_inv_count_table(lat, False), _inv_count_table(lat, True)])
    pw = pool_w.astype(BF16)
    fw = final_norm_w.reshape(1, D_MODEL)

    def mlp_casts(layer):
        return (_Cast(mlp_w1, layer, 16), _Cast(mlp_w2, layer, 16))

    x, rsb = _pack(tk, x_prompt.reshape(tk.nc, D_MODEL), x_sample.reshape(nb_lat * lat, D_MODEL))
    qkvgs = []
    mlp_w = {}
    for i in range(DEPTH):
        j = i // 2
        sh_a, sc_a, g_a, sh_m, sc_m, g_m = (mods[i, k] for k in range(N_MOD))
        nw_a = norm_mix_w[i].reshape(1, D_MODEL)
        if i % 2 == 0:
            casts = () if i in mlp_w else (_Cast(mlp_w1, i, 8), _Cast(mlp_w2, i, 8))
            x, *conv = _pool(x.reshape(1 + nb_lat, lat, D_MODEL), rsb.reshape(1 + nb_lat, lat, LANES),
                             inv, nw_a, sc_a, sh_a, g_a, pool_scale[j].reshape(1, D_MODEL),
                             pool_b[j].reshape(1, D_MODEL), pw, j, casts=casts)
            if conv:
                mlp_w[i] = conv
            x = x.reshape(tk.t, D_MODEL)
            host_casts = (_Cast(ret_w_in, j, 16),)
        else:
            later = [l for l in (i, i + 1) if l < DEPTH]
            qkvg, *conv = _inproj(tk, x, nw_a, sc_a, sh_a, w_in, cos, sin,
                                  casts=sum((mlp_casts(l) for l in later), ()))
            for n, l in enumerate(later):
                mlp_w[l] = conv[2 * n:2 * n + 2]
            qkvgs.append(qkvg)
            go_c, = _retcore(qkvg.reshape(tk.t // seq, seq, RET_IN), 0, nb_ctx, ret_decay[j],
                             ret_gn_w[j], None, j, RET_HEADS)
            go_l, w_out = _retcore(qkvg.reshape(tk.t // lat, lat, RET_IN), tk.nc // lat, nb_lat,
                                   ret_decay[j], ret_gn_w[j], state_ret, j, 1,
                                   casts=(_Cast(ret_w_out, j, 8),))
            x = _outproj(tk, x, go_c.reshape(tk.nc, RET_HV), go_l.reshape(nb_lat * lat, RET_HV),
                         w_out, g_a)
            host_casts = ()
        w1, w2 = mlp_w[i]
        x, rsb, *conv = _mlp(tk_mlp, x, norm_mlp_w[i].reshape(1, D_MODEL), sc_m, sh_m, g_m, w1, w2, fw,
                             2048, i == DEPTH - 1, casts=host_casts)
        if conv:
            w_in, = conv
    y_prompt, y_sample = x, rsb
    new_state = _states(qkvgs[0].reshape(tk.t // seq, seq, RET_IN),
                        qkvgs[1].reshape(tk.t // seq, seq, RET_IN), nb_ctx, ret_decay)
    return (y_prompt.reshape(x_prompt.shape), y_sample.reshape(x_sample.shape), new_state)
```

```python
import functools

import jax
import jax.numpy as jnp
from jax import lax
from jax.experimental import pallas as pl
from jax.experimental.pallas import tpu as pltpu

F32 = jnp.float32
BF16 = jnp.bfloat16

D_MODEL = 1024
DEPTH = 4
GRID_W = 64
POOL_WINDOWS = (2, 4, 8, 16)
POOL_GC = 256
RET_HEADS = 4
RET_DK = 256
RET_DV = 512
RET_HK = RET_HEADS * RET_DK
RET_HV = RET_HEADS * RET_DV
RET_IN = 2 * RET_HK + 2 * RET_HV
D_FF = 4 * D_MODEL
ROPE_BASE = 10000.0
NORM_EPS = 1e-6
GN_EPS = 1e-5
N_MOD = 6

LANES = 128
SLAB = 256
TM = 1024
TM_MLP = 1024
VMEM_LIMIT = 56 * 1024 * 1024


def _cparams(sem):
    return pltpu.CompilerParams(dimension_semantics=sem, vmem_limit_bytes=VMEM_LIMIT)


def _norm_mod(x, nw, sc, sh):
    ms = jnp.mean(x * x, axis=-1, keepdims=True)
    return (x * lax.rsqrt(ms + NORM_EPS)) * nw * (1.0 + sc) + sh


def _rsqrt_ms_lanes(x):
    rs = lax.rsqrt(jnp.mean(x * x, axis=-1, keepdims=True) + NORM_EPS)
    return jnp.broadcast_to(rs, (x.shape[0], LANES))


class _Tokens:
    def __init__(self, nc, lat, nlat, tm):
        assert nc % tm == 0 and lat % tm == 0
        self.nc, self.lat, self.nlat, self.tm = nc, lat, nlat, tm
        self.t = nc + lat * nlat
        self.n_ctx = nc // tm
        self.nt = self.t // tm

    def mod_row(self, i):
        return jnp.where(i < self.n_ctx, 0, 1 + (i - self.n_ctx) // (self.lat // self.tm))

    def ctx_tile(self, i):
        return jnp.minimum(i, self.n_ctx - 1)

    def lat_tile(self, i):
        return jnp.maximum(i - self.n_ctx, 0)


class _Cast:
    def __init__(self, w, layer, nblk):
        _, r, c = w.shape
        assert r % (16 * nblk) == 0
        self.w, self.layer, self.nblk, self.rows, self.cols = w, layer, nblk, r // nblk, c
        self.out_shape = jax.ShapeDtypeStruct((r, c), BF16)

    def specs(self, step):
        def blk(*ids):
            return jnp.minimum(step(*ids), self.nblk - 1)
        return (pl.BlockSpec((None, self.rows, self.cols), lambda *ids: (self.layer, blk(*ids), 0)),
                pl.BlockSpec((self.rows, self.cols), lambda *ids: (blk(*ids), 0)))


def _do_casts(src_refs, dst_refs, nblks, step):
    for src_ref, dst_ref, nblk in zip(src_refs, dst_refs, nblks):
        @pl.when(step < nblk)
        def _(src_ref=src_ref, dst_ref=dst_ref):
            dst_ref[...] = src_ref[...].astype(BF16)


def _mods_kernel(c_ref, w_ref, b_ref, o_ref):
    c = c_ref[...]
    s = c * jax.nn.sigmoid(c)
    w = w_ref[0].astype(BF16)
    o_ref[0] = jnp.dot(s.astype(BF16), w, preferred_element_type=F32) + b_ref[0]


def _mods(cond8, w_ada, b_ada):
    tn = 1024
    n = N_MOD * D_MODEL
    return pl.pallas_call(
        _mods_kernel,
        out_shape=jax.ShapeDtypeStruct((DEPTH, 8, n), F32),
        grid=(DEPTH, n // tn),
        in_specs=[
            pl.BlockSpec((8, D_MODEL), lambda l, j: (0, 0)),
            pl.BlockSpec((1, D_MODEL, tn), lambda l, j: (l, 0, j)),
            pl.BlockSpec((1, 1, tn), lambda l, j: (l, 0, j)),
        ],
        out_specs=pl.BlockSpec((1, 8, tn), lambda l, j: (l, 0, j)),
        compiler_params=_cparams(("parallel", "parallel")),
        name="mods",
    )(cond8, w_ada, b_ada.reshape(DEPTH, 1, n))


def _rms_kernel(xc_ref, xl_ref, rs_ref, *, n_ctx):
    i = pl.program_id(0)

    @pl.when(i < n_ctx)
    def _():
        rs_ref[...] = _rsqrt_ms_lanes(xc_ref[...])

    @pl.when(i >= n_ctx)
    def _():
        rs_ref[...] = _rsqrt_ms_lanes(xl_ref[...])


def _rms(tk, xc, xl):
    return pl.pallas_call(
        functools.partial(_rms_kernel, n_ctx=tk.n_ctx),
        out_shape=jax.ShapeDtypeStruct((tk.t, LANES), F32),
        grid=(tk.nt,),
        in_specs=[pl.BlockSpec((tk.tm, D_MODEL), lambda i: (tk.ctx_tile(i), 0)),
                  pl.BlockSpec((tk.tm, D_MODEL), lambda i: (tk.lat_tile(i), 0))],
        out_specs=pl.BlockSpec((tk.tm, LANES), lambda i: (i, 0)),
        compiler_params=_cparams(("arbitrary",)),
        name="rms",
    )(xc, xl)


def _window_count(pos, w, length):
    half = w // 2
    return jnp.minimum(pos - half + w, length) - jnp.maximum(pos - half, 0)


def _inv_count_table(lb, grid):
    t = jnp.arange(lb)
    rows = []
    for w in POOL_WINDOWS:
        if grid:
            cnt = _window_count(t // GRID_W, w, lb // GRID_W) * _window_count(t % GRID_W, w, GRID_W)
        else:
            cnt = _window_count(t % SLAB, w, SLAB)
        rows.append(1.0 / cnt.astype(F32))
    return jnp.broadcast_to(jnp.stack(rows)[:, :, None], (len(POOL_WINDOWS), lb, LANES))


def _pool_group(w, grid, x_ref, rs_ref, inv_ref, nw_ref, sc_ref, sh_ref, ga_ref, ps_ref, pb_ref,
                pw_ref, o_ref, colp_ref, hbuf_ref):
    half = w // 2
    nslab = x_ref.shape[1] // SLAB
    t_i = lax.broadcasted_iota(jnp.int32, (SLAB, SLAB), 0)
    s_i = lax.broadcasted_iota(jnp.int32, (SLAB, SLAB), 1)
    diff = s_i - t_i
    band = (diff >= -half) & (diff <= w - half - 1)
    if grid:
        band = band & ((s_i >> 6) == (t_i >> 6))
    sm = jnp.where(band, 1.0, 0.0).astype(BF16)

    def inv_cnt(r0):
        v = inv_ref[pl.ds(r0, SLAB), :]
        return jnp.concatenate([v, v], axis=1)

    a = nw_ref[...] * (1.0 + sc_ref[0])
    sh = sh_ref[0]
    scale_out = ga_ref[0] * ps_ref[...]
    pw = pw_ref[...]
    pb = pb_ref[...]

    def slab_h(r0):
        x = x_ref[0, pl.ds(r0, SLAB), :]
        rs = rs_ref[0, pl.ds(r0, SLAB), :]
        rs2 = jnp.concatenate([rs, rs], axis=1)
        return x, x * rs2 * a + sh

    def window_sum(h):
        hi = h.astype(BF16)
        lo = (h - hi.astype(F32)).astype(BF16)
        return (jnp.dot(sm, hi, preferred_element_type=F32)
                + jnp.dot(sm, lo, preferred_element_type=F32))

    def finish(r0, x, h, m):
        d = (m - h).astype(BF16)
        mix = jnp.dot(d, pw, preferred_element_type=F32) + pb
        o_ref[0, pl.ds(r0, SLAB), :] = x + scale_out * mix

    if not grid:
        def seq_body(s, carry):
            r0 = pl.multiple_of(s * SLAB, SLAB)
            x, h = slab_h(r0)
            finish(r0, x, h, window_sum(h) * inv_cnt(r0))
            return carry

        lax.fori_loop(0, nslab, seq_body, 0, unroll=2)
        return

    pad = 8 * GRID_W
    zeros = jnp.zeros((pad, POOL_GC), F32)
    colp_ref[pl.ds(0, pad), :] = zeros
    colp_ref[pl.ds(pad + nslab * SLAB, pad), :] = zeros

    def col_body(s, carry):
        r0 = pl.multiple_of(s * SLAB, SLAB)
        _, h = slab_h(r0)
        hbuf_ref[pl.ds(r0, SLAB), :] = h
        colp_ref[pl.ds(pad + r0, SLAB), :] = window_sum(h)
        return carry

    lax.fori_loop(0, nslab, col_body, 0, unroll=2)

    def row_body(s, carry):
        r0 = pl.multiple_of(s * SLAB, SLAB)
        acc = colp_ref[pl.ds(pad + r0 - half * GRID_W, SLAB), :]
        for j in range(1, w):
            acc = acc + colp_ref[pl.ds(pad + r0 + (j - half) * GRID_W, SLAB), :]
        finish(r0, x_ref[0, pl.ds(r0, SLAB), :], hbuf_ref[pl.ds(r0, SLAB), :], acc * inv_cnt(r0))
        return carry

    lax.fori_loop(0, nslab, row_body, 0, unroll=2)


def _pool_kernel(xc_ref, xl_ref, rs_ref, inv_ref, nw_ref, sc_ref, sh_ref, ga_ref, ps_ref, pb_ref, pw_ref,
                 *rest, cast_nblk):
    n_cast = len(cast_nblk)
    cast_in, o_ref, cast_out = rest[:n_cast], rest[n_cast], rest[n_cast + 1:2 * n_cast + 1]
    colp_ref, hbuf_ref = rest[2 * n_cast + 1:]
    b = pl.program_id(0)
    g = pl.program_id(1)
    _do_casts(cast_in, cast_out, cast_nblk, b * pl.num_programs(1) + g)
    for gi, w in enumerate(POOL_WINDOWS):
        for grid in (False, True):
            @pl.when((g == gi) & ((b > 0) if grid else (b == 0)))
            def _(w=w, grid=grid):
                _pool_group(w, grid, xl_ref if grid else xc_ref, rs_ref, inv_ref, nw_ref, sc_ref, sh_ref,
                            ga_ref, ps_ref, pb_ref, pw_ref, o_ref, colp_ref, hbuf_ref)


def _pool(xc3, xl3, xl_off, rsb3, inv, nw, sc, sh, ga, ps, pb, pw_bf16, layer_j, casts=()):
    nb, lb, _ = rsb3.shape
    ng = len(POOL_WINDOWS)
    cast_specs = [cs.specs(lambda b, g: b * ng + g) for cs in casts]
    vec = pl.BlockSpec((1, POOL_GC), lambda b, g: (0, g))
    mod = pl.BlockSpec((1, 1, POOL_GC), lambda b, g: (b, 0, g))
    return pl.pallas_call(
        functools.partial(_pool_kernel, cast_nblk=tuple(cs.nblk for cs in casts)),
        out_shape=(jax.ShapeDtypeStruct((nb, lb, D_MODEL), F32),) + tuple(cs.out_shape for cs in casts),
        grid=(nb, ng),
        in_specs=[
            pl.BlockSpec((1, lb, POOL_GC), lambda b, g: (0, 0, g)),
            pl.BlockSpec((1, lb, POOL_GC), lambda b, g: (jnp.maximum(b, 1) - xl_off, 0, g)),
            pl.BlockSpec((1, lb, LANES), lambda b, g: (b, 0, 0)),
            pl.BlockSpec((None, None, lb, LANES), lambda b, g: (jnp.minimum(b, 1), g, 0, 0)),
            vec, mod, mod, mod, vec, vec,
            pl.BlockSpec((None, None, POOL_GC, POOL_GC), lambda b, g: (layer_j, g, 0, 0)),
        ] + [s[0] for s in cast_specs],
        out_specs=(pl.BlockSpec((1, lb, POOL_GC), lambda b, g: (b, 0, g)),)
        + tuple(s[1] for s in cast_specs),
        scratch_shapes=[pltpu.VMEM((lb + 16 * GRID_W, POOL_GC), F32), pltpu.VMEM((lb, POOL_GC), F32)],
        compiler_params=_cparams(("arbitrary", "arbitrary")),
        name="pool",
    )(xc3, xl3, rsb3, inv, nw, sc, sh, ga, ps, pb, pw_bf16, *[cs.w for cs in casts])


def _mlp_kernel(x_ref, nw_ref, sc_ref, sh_ref, g_ref, w1_ref, w2_ref, fw_ref, *rest, final, fc, n_ctx,
                cast_nblk):
    n_cast = len(cast_nblk)
    cast_in, rest = rest[:n_cast], rest[n_cast:]
    outs, (h_ref, acc_ref) = rest[:2 + n_cast], rest[2 + n_cast:]
    if final:
        yc_ref, yl_ref = outs[:2]
    else:
        o_ref, rs_ref = outs[:2]
    i = pl.program_id(0)
    j = pl.program_id(1)
    nj = pl.num_programs(1)

    _do_casts(cast_in, outs[2:], cast_nblk, i * nj + j)

    def ffn(h):
        out = None
        for c0 in range(0, w1_ref.shape[1], fc):
            a = jnp.dot(h, w1_ref[:, c0:c0 + fc], preferred_element_type=F32)
            a = jnp.maximum(a, 0.0)
            d = jnp.dot((a * a).astype(BF16), w2_ref[c0:c0 + fc, :], preferred_element_type=F32)
            out = d if out is None else out + d
        return out

    @pl.when(j == 0)
    def _():
        h = _norm_mod(x_ref[...], nw_ref[...], sc_ref[0], sh_ref[0]).astype(BF16)
        h_ref[...] = h
        acc_ref[...] = ffn(h)

    @pl.when((j > 0) & (j < nj - 1))
    def _():
        acc_ref[...] += ffn(h_ref[...])

    @pl.when(j == nj - 1)
    def _():
        y = x_ref[...] + g_ref[0] * (acc_ref[...] + ffn(h_ref[...]))
        if not final:
            o_ref[...] = y
            rs_ref[...] = _rsqrt_ms_lanes(y)
            return
        rs = lax.rsqrt(jnp.mean(y * y, axis=-1, keepdims=True) + NORM_EPS)
        y = (y * rs) * fw_ref[...]

        @pl.when(i < n_ctx)
        def _():
            yc_ref[...] = y

        @pl.when(i >= n_ctx)
        def _():
            yl_ref[...] = y


def _mlp(tk, x2d, nw, sc, sh, g, w1, w2, fw, tf, final, casts=()):
    nj = D_FF // tf
    assert nj >= 2
    cast_specs = [cs.specs(lambda i, j: i * nj + j) for cs in casts]
    vec = pl.BlockSpec((1, D_MODEL), lambda i, j: (0, 0))
    mod = pl.BlockSpec((1, 1, D_MODEL), lambda i, j: (tk.mod_row(i), 0, 0))
    if final:
        out_shape = (jax.ShapeDtypeStruct((tk.nc, D_MODEL), F32),
                     jax.ShapeDtypeStruct((tk.t - tk.nc, D_MODEL), F32))
        out_specs = (pl.BlockSpec((tk.tm, D_MODEL), lambda i, j: (tk.ctx_tile(i), 0)),
                     pl.BlockSpec((tk.tm, D_MODEL), lambda i, j: (tk.lat_tile(i), 0)))
    else:
        out_shape = (jax.ShapeDtypeStruct((tk.t, D_MODEL), F32), jax.ShapeDtypeStruct((tk.t, LANES), F32))
        out_specs = (pl.BlockSpec((tk.tm, D_MODEL), lambda i, j: (i, 0)),
                     pl.BlockSpec((tk.tm, LANES), lambda i, j: (i, 0)))
    return pl.pallas_call(
        functools.partial(_mlp_kernel, final=final, fc=min(tf, 1024), n_ctx=tk.n_ctx,
                          cast_nblk=tuple(cs.nblk for cs in casts)),
        out_shape=out_shape + tuple(cs.out_shape for cs in casts),
        grid=(tk.nt, nj),
        in_specs=[
            pl.BlockSpec((tk.tm, D_MODEL), lambda i, j: (i, 0)),
            vec, mod, mod, mod,
            pl.BlockSpec((D_MODEL, tf), lambda i, j: (0, j)),
            pl.BlockSpec((tf, D_MODEL), lambda i, j: (j, 0)),
            vec,
        ] + [s[0] for s in cast_specs],
        out_specs=out_specs + tuple(s[1] for s in cast_specs),
        scratch_shapes=[pltpu.VMEM((tk.tm, D_MODEL), BF16), pltpu.VMEM((tk.tm, D_MODEL), F32)],
        compiler_params=_cparams(("arbitrary", "arbitrary")),
        name="mlp",
    )(x2d, nw, sc, sh, g, w1, w2, fw, *[cs.w for cs in casts])


def _inproj_kernel(x_ref, nw_ref, sc_ref, sh_ref, w_ref, cos_ref, sin_ref, *rest, tc, n_ctx, cast_nblk):
    n_cast = len(cast_nblk)
    cast_in, o_ref, cast_out, h_ref = (rest[:n_cast], rest[n_cast], rest[n_cast + 1:2 * n_cast + 1],
                                       rest[2 * n_cast + 1])
    i = pl.program_id(0)
    j = pl.program_id(1)
    nchunk = o_ref.shape[1] // tc
    _do_casts(cast_in, cast_out, cast_nblk, i * pl.num_programs(1) + j)

    def qk_step(rope):
        h = _norm_mod(x_ref[...], nw_ref[...], sc_ref[0], sh_ref[0]).astype(BF16)
        h_ref[...] = h
        for c in range(nchunk):
            p = jnp.dot(h, w_ref[:, c * tc:(c + 1) * tc], preferred_element_type=F32)
            if c * tc >= RET_HK:
                p = p * (RET_DK ** -0.5)
            if not rope:
                o_ref[:, c * tc:(c + 1) * tc] = p.astype(BF16)
                continue
            for t in range(tc // LANES):
                pt = p[:, t * LANES:(t + 1) * LANES]
                c0 = (t % 2) * LANES
                rot = pltpu.roll(pt, LANES // 2, axis=1)
                pt = pt * cos_ref[:, c0:c0 + LANES] + rot * sin_ref[:, c0:c0 + LANES]
                o_ref[:, c * tc + t * LANES:c * tc + (t + 1) * LANES] = pt.astype(BF16)

    pl.when((j == 0) & (i < n_ctx))(lambda: qk_step(False))
    pl.when((j == 0) & (i >= n_ctx))(lambda: qk_step(True))

    @pl.when(j > 0)
    def _():
        h = h_ref[...]
        for c in range(nchunk):
            p = jnp.dot(h, w_ref[:, c * tc:(c + 1) * tc], preferred_element_type=F32)
            o_ref[:, c * tc:(c + 1) * tc] = p.astype(BF16)


def _inproj(tk, x2d, nw, sc, sh, w_in, cos, sin, casts=()):
    n_tab = cos.shape[0] // tk.tm
    tn = 2 * RET_HK
    nj = RET_IN // tn
    assert RET_HV == tn and RET_IN == 3 * tn
    cast_specs = [cs.specs(lambda i, j: i * nj + j) for cs in casts]
    vec = pl.BlockSpec((1, D_MODEL), lambda i, j: (0, 0))
    mod = pl.BlockSpec((1, 1, D_MODEL), lambda i, j: (tk.mod_row(i), 0, 0))
    tab = pl.BlockSpec((tk.tm, RET_DK), lambda i, j: (tk.lat_tile(i) % n_tab, 0))
    return pl.pallas_call(
        functools.partial(_inproj_kernel, tc=512, n_ctx=tk.n_ctx,
                          cast_nblk=tuple(cs.nblk for cs in casts)),
        out_shape=(jax.ShapeDtypeStruct((tk.t, RET_IN), BF16),) + tuple(cs.out_shape for cs in casts),
        grid=(tk.nt, nj),
        in_specs=[
            pl.BlockSpec((tk.tm, D_MODEL), lambda i, j: (i, 0)),
            vec, mod, mod,
            pl.BlockSpec((D_MODEL, tn), lambda i, j: (0, j)),
            tab, tab,
        ] + [s[0] for s in cast_specs],
        out_specs=(pl.BlockSpec((tk.tm, tn), lambda i, j: (i, j)),) + tuple(s[1] for s in cast_specs),
        scratch_shapes=[pltpu.VMEM((tk.tm, D_MODEL), BF16)],
        compiler_params=_cparams(("arbitrary", "arbitrary")),
        name="inproj",
    )(x2d, nw, sc, sh, w_in, cos, sin, *[cs.w for cs in casts])


def _log_gamma(decay):
    return jnp.log1p(-jnp.exp2(-jnp.full((1, 1), decay, F32)))


def _dot_tn(a, b):
    return lax.dot_general(a, b, (((0,), (0,)), ((), ())), preferred_element_type=F32)


def _dot_nt(a, b):
    return lax.dot_general(a, b, (((1,), (1,)), ((), ())), preferred_element_type=F32)


def _ret_kernel(dec_ref, q_ref, k_ref, v_ref, g_ref, gnw_ref, *rest, heads, has_state, emit_states,
                cast_nblk):
    n_cast = len(cast_nblk)
    if has_state:
        s0_ref, rest = rest[0], rest[1:]
    if emit_states:
        (kp_ref, vp_ref, decall_ref), rest = rest[:3], rest[3:]
    cast_in, o_ref, rest = rest[:n_cast], rest[n_cast], rest[n_cast + 1:]
    if emit_states:
        st_ref, rest = rest[0], rest[1:]
    cast_out, rest = rest[:n_cast], rest[n_cast:]
    if has_state:
        oacc_ref, sf_ref, sb_ref = rest
    _do_casts(cast_in, cast_out, cast_nblk, pl.program_id(0) * pl.num_programs(1) + pl.program_id(1))
    c = SLAB
    nchunks = q_ref.shape[1] // c
    row_k = lax.broadcasted_iota(jnp.int32, (c, RET_DK), 0).astype(F32)
    i_f = lax.broadcasted_iota(jnp.int32, (c, c), 0).astype(F32)
    j_f = lax.broadcasted_iota(jnp.int32, (c, c), 1).astype(F32)
    dif = i_f - j_f

    for hh in range(heads):
        head = pl.program_id(1) * heads + hh
        lgf = _log_gamma(dec_ref[0, head])
        lgb = _log_gamma(dec_ref[1, head])
        dmat = jnp.exp(lgf * jnp.maximum(dif, 0.0) + lgb * jnp.maximum(-dif, 0.0))
        gnw = gnw_ref[head]
        ks = slice(hh * RET_DK, (hh + 1) * RET_DK)
        vs = slice(hh * RET_DV, (hh + 1) * RET_DV)

        def intra(r0):
            qc = q_ref[0, pl.ds(r0, c), ks]
            kc = k_ref[0, pl.ds(r0, c), ks]
            vc = v_ref[0, pl.ds(r0, c), vs]
            p = (_dot_nt(qc, kc) * dmat).astype(BF16)
            return qc, kc, vc, jnp.dot(p, vc, preferred_element_type=F32)

        def finalize(r0, o):
            mu = jnp.mean(o, axis=-1, keepdims=True)
            oc = o - mu
            var = jnp.mean(oc * oc, axis=-1, keepdims=True)
            y = oc * lax.rsqrt(var + GN_EPS) * gnw
            hg = 0.5 * g_ref[0, pl.ds(r0, c), vs].astype(F32)
            o_ref[0, pl.ds(r0, c), vs] = ((hg + hg * jnp.tanh(hg)) * y).astype(BF16)

        if emit_states:
            l = nchunks * c
            row_l = lax.broadcasted_iota(jnp.int32, (l, RET_DK), 0).astype(F32)
            for jj, (kk_ref, vv_ref) in enumerate(((kp_ref, vp_ref), (k_ref, v_ref))):
                lsf = _log_gamma(decall_ref[jj, 0, head])
                lsb = _log_gamma(decall_ref[jj, 1, head])
                kk = kk_ref[0, :, ks].astype(F32)
                vv = vv_ref[0, :, vs]
                st_ref[0, jj, 0, hh] = _dot_tn((kk * jnp.exp(lsf * (l - 1.0 - row_l))).astype(BF16), vv)
                st_ref[0, jj, 1, hh] = _dot_tn((kk * jnp.exp(lsb * row_l)).astype(BF16), vv)

        if not has_state:
            for ci in range(nchunks):
                _, _, _, o = intra(ci * c)
                finalize(ci * c, o)
            continue

        xi_f = jnp.exp(lgf * (row_k + 1.0))
        xi_b = jnp.exp(lgb * (c - row_k))
        zeta_f = jnp.exp(lgf * (c - 1.0 - row_k))
        zeta_b = jnp.exp(lgb * row_k)
        gc_f = jnp.exp(lgf * c)
        gc_b = jnp.exp(lgb * c)
        sf_ref[...] = s0_ref[0, 0, 0, 0]
        sb_ref[...] = s0_ref[0, 0, 1, 0]

        def fwd_part(r0):
            qc, kc, vc, o = intra(r0)
            qx = (qc.astype(F32) * xi_f).astype(BF16)
            o = o + jnp.dot(qx, sf_ref[...].astype(BF16), preferred_element_type=F32)
            kz = (kc.astype(F32) * zeta_f).astype(BF16)
            sf_ref[...] = gc_f * sf_ref[...] + _dot_tn(kz, vc)
            return o

        def bwd_part(r0):
            qc = q_ref[0, pl.ds(r0, c), ks]
            kc = k_ref[0, pl.ds(r0, c), ks]
            vc = v_ref[0, pl.ds(r0, c), vs]
            qx = (qc.astype(F32) * xi_b).astype(BF16)
            o = jnp.dot(qx, sb_ref[...].astype(BF16), preferred_element_type=F32)
            kz = (kc.astype(F32) * zeta_b).astype(BF16)
            sb_ref[...] = gc_b * sb_ref[...] + _dot_tn(kz, vc)
            return o

        def first_half(t, carry):
            rf = pl.multiple_of(t * c, c)
            rb = pl.multiple_of((nchunks - 1 - t) * c, c)
            oacc_ref[pl.ds(rf, c), :] = fwd_part(rf)
            oacc_ref[pl.ds(rb, c), :] = bwd_part(rb)
            return carry

        def second_half(t, carry):
            rf = pl.multiple_of(t * c, c)
            rb = pl.multiple_of((nchunks - 1 - t) * c, c)
            finalize(rf, oacc_ref[pl.ds(rf, c), :] + fwd_part(rf))
            finalize(rb, oacc_ref[pl.ds(rb, c), :] + bwd_part(rb))
            return carry

        assert nchunks % 2 == 0
        lax.fori_loop(0, nchunks // 2, first_half, 0)
        lax.fori_loop(nchunks // 2, nchunks, second_half, 0)


def _retcore(qkvg, b_off, nb, decay, gnw, state, layer_j, heads, casts=(), states_of=None):
    _, l, _ = qkvg.shape
    has_state = state is not None
    nh = RET_HEADS // heads
    q_off = 0
    k_off = RET_HK // (RET_DK * heads)
    v_off = 2 * RET_HK // (RET_DV * heads)
    g_off = (2 * RET_HK + RET_HV) // (RET_DV * heads)
    in_specs = [
        pl.BlockSpec(memory_space=pltpu.SMEM),
        pl.BlockSpec((1, l, RET_DK * heads), lambda i, h: (b_off + i, 0, q_off + h)),
        pl.BlockSpec((1, l, RET_DK * heads), lambda i, h: (b_off + i, 0, k_off + h)),
        pl.BlockSpec((1, l, RET_DV * heads), lambda i, h: (b_off + i, 0, v_off + h)),
        pl.BlockSpec((1, l, RET_DV * heads), lambda i, h: (b_off + i, 0, g_off + h)),
        pl.BlockSpec((RET_HEADS, 1, RET_DV), lambda i, h: (0, 0, 0)),
    ]
    args = [decay, qkvg, qkvg, qkvg, qkvg, gnw.reshape(RET_HEADS, 1, RET_DV)]
    scratch = []
    if has_state:
        assert heads == 1
        in_specs.append(pl.BlockSpec((1, 1, 2, 1, RET_DK, RET_DV),
                                     lambda i, h: (i, layer_j, 0, h, 0, 0)))
        args.append(state)
        scratch = [pltpu.VMEM((l, RET_DV), F32), pltpu.VMEM((RET_DK, RET_DV), F32),
                   pltpu.VMEM((RET_DK, RET_DV), F32)]
    out_shape = [jax.ShapeDtypeStruct((nb, l, RET_HV), BF16)]
    out_specs = [pl.BlockSpec((1, l, RET_DV * heads), lambda i, h: (i, 0, h))]
    if states_of is not None:
        qkvg_prev, decay_all = states_of
        assert heads == RET_HEADS and not has_state and decay_all.shape[0] == 2
        in_specs += [pl.BlockSpec((1, l, RET_HK), lambda i, h: (b_off + i, 0, 1)),
                     pl.BlockSpec((1, l, RET_HV), lambda i, h: (b_off + i, 0, 1)),
                     pl.BlockSpec(memory_space=pltpu.SMEM)]
        args += [qkvg_prev, qkvg_prev, decay_all]
        out_shape.append(jax.ShapeDtypeStruct((nb, 2, 2, RET_HEADS, RET_DK, RET_DV), F32))
        out_specs.append(pl.BlockSpec((1, 2, 2, RET_HEADS, RET_DK, RET_DV),
                                      lambda i, h: (i, 0, 0, 0, 0, 0)))
    cast_specs = [cs.specs(lambda i, h: i * nh + h) for cs in casts]
    return pl.pallas_call(
        functools.partial(_ret_kernel, heads=heads, has_state=has_state,
                          emit_states=states_of is not None,
                          cast_nblk=tuple(cs.nblk for cs in casts)),
        out_shape=tuple(out_shape) + tuple(cs.out_shape for cs in casts),
        grid=(nb, nh),
        in_specs=in_specs + [s[0] for s in cast_specs],
        out_specs=tuple(out_specs) + tuple(s[1] for s in cast_specs),
        scratch_shapes=scratch,
        compiler_params=_cparams(("arbitrary", "arbitrary")),
        name="retcore_state" if has_state else "retcore",
    )(*args, *[cs.w for cs in casts])


def _outproj_kernel(x_ref, goc_ref, gol_ref, w_ref, g_ref, o_ref, *, n_ctx):
    i = pl.program_id(0)

    def emit(go_ref):
        mix = jnp.dot(go_ref[...], w_ref[...], preferred_element_type=F32)
        o_ref[...] = x_ref[...] + g_ref[0] * mix

    pl.when(i < n_ctx)(lambda: emit(goc_ref))
    pl.when(i >= n_ctx)(lambda: emit(gol_ref))


def _outproj(tk, x2d, go_c, go_l, w_out, ga):
    return pl.pallas_call(
        functools.partial(_outproj_kernel, n_ctx=tk.n_ctx),
        out_shape=jax.ShapeDtypeStruct((tk.t, D_MODEL), F32),
        grid=(tk.nt,),
        in_specs=[
            pl.BlockSpec((tk.tm, D_MODEL), lambda i: (i, 0)),
            pl.BlockSpec((tk.tm, RET_HV), lambda i: (tk.ctx_tile(i), 0)),
            pl.BlockSpec((tk.tm, RET_HV), lambda i: (tk.lat_tile(i), 0)),
            pl.BlockSpec((RET_HV, D_MODEL), lambda i: (0, 0)),
            pl.BlockSpec((1, 1, D_MODEL), lambda i: (tk.mod_row(i), 0, 0)),
        ],
        out_specs=pl.BlockSpec((tk.tm, D_MODEL), lambda i: (i, 0)),
        compiler_params=_cparams(("arbitrary",)),
        name="outproj",
    )(x2d, go_c, go_l, w_out, ga)


def _rope_tables(length):
    quarter = RET_DK // 4
    half = RET_DK // 2
    rows = length // GRID_W
    freqs = ROPE_BASE ** (-jnp.arange(quarter, dtype=F32) / quarter)
    sign = jnp.concatenate([-jnp.ones((quarter,), F32), jnp.ones((quarter,), F32)])

    def tabs(npos):
        ang = jnp.arange(npos, dtype=F32)[:, None] * freqs[None, :]
        return (jnp.concatenate([jnp.cos(ang), jnp.cos(ang)], axis=-1),
                jnp.concatenate([jnp.sin(ang), jnp.sin(ang)], axis=-1) * sign)

    def by_row(tab):
        return jnp.broadcast_to(tab[:, None, :], (rows, GRID_W, half)).reshape(length, half)

    def by_col(tab):
        return jnp.broadcast_to(tab[None, :, :], (rows, GRID_W, half)).reshape(length, half)

    cos_r, sin_r = tabs(rows)
    cos_c, sin_c = tabs(GRID_W)
    return (jnp.concatenate([by_row(cos_r), by_col(cos_c)], axis=-1),
            jnp.concatenate([by_row(sin_r), by_col(sin_c)], axis=-1))


def kernel(x_prompt, x_sample, state_ret, c, c_ctx, w_ada, b_ada, norm_mix_w, norm_mlp_w, pool_w,
           pool_b, pool_scale, ret_w_in, ret_decay, ret_gn_w, ret_w_out, mlp_w1, mlp_w2,
           final_norm_w):
    nb_ctx, seq, _ = x_prompt.shape
    nb_lat, lat, _ = x_sample.shape
    tk = _Tokens(nb_ctx * seq, lat, nb_lat, TM)
    tk_mlp = _Tokens(nb_ctx * seq, lat, nb_lat, TM_MLP)
    assert tk.nc == lat and seq == SLAB

    cond8 = jnp.concatenate([c_ctx[None, :], c, jnp.zeros((8 - 1 - nb_lat, D_MODEL), F32)], axis=0)
    mods = _mods(cond8, w_ada, b_ada)
    mods = mods.reshape(DEPTH, 8, N_MOD, D_MODEL).transpose(0, 2, 1, 3)
    mods = mods[:, :, :1 + nb_lat, None, :]

    cos, sin = _rope_tables(lat)
    inv = jnp.stack([_inv_count_table(lat, False), _inv_count_table(lat, True)])
    pw = pool_w.astype(BF16)
    fw = final_norm_w.reshape(1, D_MODEL)

    def mlp_casts(layer):
        return (_Cast(mlp_w1, layer, 16), _Cast(mlp_w2, layer, 16))

    rsb = _rms(tk, x_prompt.reshape(tk.nc, D_MODEL), x_sample.reshape(nb_lat * lat, D_MODEL))
    x = None
    qkvgs = []
    mlp_w = {}
    for i in range(DEPTH):
        j = i // 2
        sh_a, sc_a, g_a, sh_m, sc_m, g_m = (mods[i, k] for k in range(N_MOD))
        nw_a = norm_mix_w[i].reshape(1, D_MODEL)
        if i % 2 == 0:
            casts = () if i in mlp_w else (_Cast(mlp_w1, i, 8), _Cast(mlp_w2, i, 8))
            if x is None:
                xc3, xl3, xl_off = x_prompt.reshape(1, tk.nc, D_MODEL), x_sample, 1
            else:
                xc3 = xl3 = x.reshape(1 + nb_lat, lat, D_MODEL)
                xl_off = 0
            x, *conv = _pool(xc3, xl3, xl_off, rsb.reshape(1 + nb_lat, lat, LANES),
                             inv, nw_a, sc_a, sh_a, g_a, pool_scale[j].reshape(1, D_MODEL),
                             pool_b[j].reshape(1, D_MODEL), pw, j, casts=casts)
            if conv:
                mlp_w[i] = conv
            x = x.reshape(tk.t, D_MODEL)
            host_casts = (_Cast(ret_w_in, j, 16),)
        else:
            later = [l for l in (i, i + 1) if l < DEPTH]
            qkvg, *conv = _inproj(tk, x, nw_a, sc_a, sh_a, w_in, cos, sin,
                                  casts=sum((mlp_casts(l) for l in later), ()))
            for n, l in enumerate(later):
                mlp_w[l] = conv[2 * n:2 * n + 2]
            qkvgs.append(qkvg)
            states_of = None
            if len(qkvgs) == 2:
                states_of = (qkvgs[0].reshape(tk.t // seq, seq, RET_IN), ret_decay)
            go_c, *new_state = _retcore(qkvg.reshape(tk.t // seq, seq, RET_IN), 0, nb_ctx, ret_decay[j],
                                        ret_gn_w[j], None, j, RET_HEADS, states_of=states_of)
            go_l, w_out = _retcore(qkvg.reshape(tk.t // lat, lat, RET_IN), tk.nc // lat, nb_lat,
                                   ret_decay[j], ret_gn_w[j], state_ret, j, 1,
                                   casts=(_Cast(ret_w_out, j, 8),))
            x = _outproj(tk, x, go_c.reshape(tk.nc, RET_HV), go_l.reshape(nb_lat * lat, RET_HV),
                         w_out, g_a)
            host_casts = ()
        w1, w2 = mlp_w[i]
        x, rsb, *conv = _mlp(tk_mlp, x, norm_mlp_w[i].reshape(1, D_MODEL), sc_m, sh_m, g_m, w1, w2, fw,
                             2048, i == DEPTH - 1, casts=host_casts)
        if conv:
            w_in, = conv
    y_prompt, y_sample = x, rsb
    return (y_prompt.reshape(x_prompt.shape), y_sample.reshape(x_sample.shape), new_state[0])
```

```python
import functools

import jax
import jax.numpy as jnp
from jax import lax
from jax.experimental import pallas as pl
from jax.experimental.pallas import tpu as pltpu

F32 = jnp.float32
BF16 = jnp.bfloat16

D_MODEL = 1024
DEPTH = 4
GRID_W = 64
POOL_WINDOWS = (2, 4, 8, 16)
POOL_GC = 256
RET_HEADS = 4
RET_DK = 256
RET_DV = 512
RET_HK = RET_HEADS * RET_DK
RET_HV = RET_HEADS * RET_DV
RET_IN = 2 * RET_HK + 2 * RET_HV
D_FF = 4 * D_MODEL
ROPE_BASE = 10000.0
NORM_EPS = 1e-6
GN_EPS = 1e-5
N_MOD = 6

LANES = 128
SLAB = 256
TM = 1024
TM_MLP = 1024
VMEM_LIMIT = 56 * 1024 * 1024


def _cparams(sem):
    return pltpu.CompilerParams(dimension_semantics=sem, vmem_limit_bytes=VMEM_LIMIT)


def _norm_mod(x, nw, sc, sh):
    ms = jnp.mean(x * x, axis=-1, keepdims=True)
    return (x * lax.rsqrt(ms + NORM_EPS)) * nw * (1.0 + sc) + sh


def _rsqrt_ms_lanes(x):
    rs = lax.rsqrt(jnp.mean(x * x, axis=-1, keepdims=True) + NORM_EPS)
    return jnp.broadcast_to(rs, (x.shape[0], LANES))


class _Tokens:
    def __init__(self, nc, lat, nlat, tm):
        assert nc % tm == 0 and lat % tm == 0
        self.nc, self.lat, self.nlat, self.tm = nc, lat, nlat, tm
        self.t = nc + lat * nlat
        self.n_ctx = nc // tm
        self.nt = self.t // tm

    def mod_row(self, i):
        return jnp.where(i < self.n_ctx, 0, 1 + (i - self.n_ctx) // (self.lat // self.tm))

    def ctx_tile(self, i):
        return jnp.minimum(i, self.n_ctx - 1)

    def lat_tile(self, i):
        return jnp.maximum(i - self.n_ctx, 0)


class _Cast:
    def __init__(self, w, layer, nblk):
        _, r, c = w.shape
        assert r % (16 * nblk) == 0
        self.w, self.layer, self.nblk, self.rows, self.cols = w, layer, nblk, r // nblk, c
        self.out_shape = jax.ShapeDtypeStruct((r, c), BF16)

    def specs(self, step):
        def blk(*ids):
            return jnp.minimum(step(*ids), self.nblk - 1)
        return (pl.BlockSpec((None, self.rows, self.cols), lambda *ids: (self.layer, blk(*ids), 0)),
                pl.BlockSpec((self.rows, self.cols), lambda *ids: (blk(*ids), 0)))


def _do_casts(src_refs, dst_refs, nblks, step):
    for src_ref, dst_ref, nblk in zip(src_refs, dst_refs, nblks):
        @pl.when(step < nblk)
        def _(src_ref=src_ref, dst_ref=dst_ref):
            dst_ref[...] = src_ref[...].astype(BF16)


def _mods_kernel(c_ref, w_ref, b_ref, o_ref):
    c = c_ref[...]
    s = c * jax.nn.sigmoid(c)
    w = w_ref[0].astype(BF16)
    o_ref[0] = jnp.dot(s.astype(BF16), w, preferred_element_type=F32) + b_ref[0]


def _mods(cond8, w_ada, b_ada):
    tn = 1024
    n = N_MOD * D_MODEL
    return pl.pallas_call(
        _mods_kernel,
        out_shape=jax.ShapeDtypeStruct((DEPTH, 8, n), F32),
        grid=(DEPTH, n // tn),
        in_specs=[
            pl.BlockSpec((8, D_MODEL), lambda l, j: (0, 0)),
            pl.BlockSpec((1, D_MODEL, tn), lambda l, j: (l, 0, j)),
            pl.BlockSpec((1, 1, tn), lambda l, j: (l, 0, j)),
        ],
        out_specs=pl.BlockSpec((1, 8, tn), lambda l, j: (l, 0, j)),
        compiler_params=_cparams(("parallel", "parallel")),
        name="mods",
    )(cond8, w_ada, b_ada.reshape(DEPTH, 1, n))


def _rms_kernel(xc_ref, xl_ref, rs_ref, *, n_ctx):
    i = pl.program_id(0)

    @pl.when(i < n_ctx)
    def _():
        rs_ref[...] = _rsqrt_ms_lanes(xc_ref[...])

    @pl.when(i >= n_ctx)
    def _():
        rs_ref[...] = _rsqrt_ms_lanes(xl_ref[...])


def _rms(tk, xc, xl):
    return pl.pallas_call(
        functools.partial(_rms_kernel, n_ctx=tk.n_ctx),
        out_shape=jax.ShapeDtypeStruct((tk.t, LANES), F32),
        grid=(tk.nt,),
        in_specs=[pl.BlockSpec((tk.tm, D_MODEL), lambda i: (tk.ctx_tile(i), 0)),
                  pl.BlockSpec((tk.tm, D_MODEL), lambda i: (tk.lat_tile(i), 0))],
        out_specs=pl.BlockSpec((tk.tm, LANES), lambda i: (i, 0)),
        compiler_params=_cparams(("arbitrary",)),
        name="rms",
    )(xc, xl)


def _window_count(pos, w, length):
    half = w // 2
    return jnp.minimum(pos - half + w, length) - jnp.maximum(pos - half, 0)


def _inv_count_table(lb, grid):
    t = jnp.arange(lb)
    rows = []
    for w in POOL_WINDOWS:
        if grid:
            cnt = _window_count(t // GRID_W, w, lb // GRID_W) * _window_count(t % GRID_W, w, GRID_W)
        else:
            cnt = _window_count(t % SLAB, w, SLAB)
        rows.append(1.0 / cnt.astype(F32))
    return jnp.broadcast_to(jnp.stack(rows)[:, :, None], (len(POOL_WINDOWS), lb, LANES))


def _pool_group(w, grid, x_ref, rs_ref, inv_ref, nw_ref, sc_ref, sh_ref, ga_ref, ps_ref, pb_ref,
                pw_ref, o_ref, colp_ref, hbuf_ref):
    half = w // 2
    nslab = x_ref.shape[1] // SLAB
    t_i = lax.broadcasted_iota(jnp.int32, (SLAB, SLAB), 0)
    s_i = lax.broadcasted_iota(jnp.int32, (SLAB, SLAB), 1)
    diff = s_i - t_i
    band = (diff >= -half) & (diff <= w - half - 1)
    if grid:
        band = band & ((s_i >> 6) == (t_i >> 6))
    sm = jnp.where(band, 1.0, 0.0).astype(BF16)

    def inv_cnt(r0):
        v = inv_ref[pl.ds(r0, SLAB), :]
        return jnp.concatenate([v, v], axis=1)

    a = nw_ref[...] * (1.0 + sc_ref[0])
    sh = sh_ref[0]
    scale_out = ga_ref[0] * ps_ref[...]
    pw = pw_ref[...]
    pb = pb_ref[...]

    def slab_h(r0):
        x = x_ref[0, pl.ds(r0, SLAB), :]
        rs = rs_ref[0, pl.ds(r0, SLAB), :]
        rs2 = jnp.concatenate([rs, rs], axis=1)
        return x, x * rs2 * a + sh

    def window_sum(h):
        hi = h.astype(BF16)
        lo = (h - hi.astype(F32)).astype(BF16)
        return (jnp.dot(sm, hi, preferred_element_type=F32)
                + jnp.dot(sm, lo, preferred_element_type=F32))

    def finish(r0, x, h, m):
        d = (m - h).astype(BF16)
        mix = jnp.dot(d, pw, preferred_element_type=F32) + pb
        o_ref[0, pl.ds(r0, SLAB), :] = x + scale_out * mix

    if not grid:
        def seq_body(s, carry):
            r0 = pl.multiple_of(s * SLAB, SLAB)
            x, h = slab_h(r0)
            finish(r0, x, h, window_sum(h) * inv_cnt(r0))
            return carry

        lax.fori_loop(0, nslab, seq_body, 0, unroll=4)
        return

    pad = 8 * GRID_W
    zeros = jnp.zeros((pad, POOL_GC), F32)
    colp_ref[pl.ds(0, pad), :] = zeros
    colp_ref[pl.ds(pad + nslab * SLAB, pad), :] = zeros

    def col_body(s, carry):
        r0 = pl.multiple_of(s * SLAB, SLAB)
        _, h = slab_h(r0)
        hbuf_ref[pl.ds(r0, SLAB), :] = h
        colp_ref[pl.ds(pad + r0, SLAB), :] = window_sum(h)
        return carry

    lax.fori_loop(0, nslab, col_body, 0, unroll=4)

    def row_body(s, carry):
        r0 = pl.multiple_of(s * SLAB, SLAB)
        acc = colp_ref[pl.ds(pad + r0 - half * GRID_W, SLAB), :]
        for j in range(1, w):
            acc = acc + colp_ref[pl.ds(pad + r0 + (j - half) * GRID_W, SLAB), :]
        finish(r0, x_ref[0, pl.ds(r0, SLAB), :], hbuf_ref[pl.ds(r0, SLAB), :], acc * inv_cnt(r0))
        return carry

    lax.fori_loop(0, nslab, row_body, 0, unroll=2)


def _pool_kernel(xc_ref, xl_ref, rs_ref, inv_ref, nw_ref, sc_ref, sh_ref, ga_ref, ps_ref, pb_ref, pw_ref,
                 *rest, cast_nblk):
    n_cast = len(cast_nblk)
    cast_in, o_ref, cast_out = rest[:n_cast], rest[n_cast], rest[n_cast + 1:2 * n_cast + 1]
    colp_ref, hbuf_ref = rest[2 * n_cast + 1:]
    b = pl.program_id(0)
    g = pl.program_id(1)
    _do_casts(cast_in, cast_out, cast_nblk, b * pl.num_programs(1) + g)
    for gi, w in enumerate(POOL_WINDOWS):
        for grid in (False, True):
            @pl.when((g == gi) & ((b > 0) if grid else (b == 0)))
            def _(w=w, grid=grid):
                _pool_group(w, grid, xl_ref if grid else xc_ref, rs_ref, inv_ref, nw_ref, sc_ref, sh_ref,
                            ga_ref, ps_ref, pb_ref, pw_ref, o_ref, colp_ref, hbuf_ref)


def _pool(xc3, xl3, xl_off, rsb3, inv, nw, sc, sh, ga, ps, pb, pw_bf16, layer_j, casts=()):
    nb, lb, _ = rsb3.shape
    ng = len(POOL_WINDOWS)
    cast_specs = [cs.specs(lambda b, g: b * ng + g) for cs in casts]
    vec = pl.BlockSpec((1, POOL_GC), lambda b, g: (0, g))
    mod = pl.BlockSpec((1, 1, POOL_GC), lambda b, g: (b, 0, g))
    return pl.pallas_call(
        functools.partial(_pool_kernel, cast_nblk=tuple(cs.nblk for cs in casts)),
        out_shape=(jax.ShapeDtypeStruct((nb, lb, D_MODEL), F32),) + tuple(cs.out_shape for cs in casts),
        grid=(nb, ng),
        in_specs=[
            pl.BlockSpec((1, lb, POOL_GC), lambda b, g: (0, 0, g)),
            pl.BlockSpec((1, lb, POOL_GC), lambda b, g: (jnp.maximum(b, 1) - xl_off, 0, g)),
            pl.BlockSpec((1, lb, LANES), lambda b, g: (b, 0, 0)),
            pl.BlockSpec((None, None, lb, LANES), lambda b, g: (jnp.minimum(b, 1), g, 0, 0)),
            vec, mod, mod, mod, vec, vec,
            pl.BlockSpec((None, None, POOL_GC, POOL_GC), lambda b, g: (layer_j, g, 0, 0)),
        ] + [s[0] for s in cast_specs],
        out_specs=(pl.BlockSpec((1, lb, POOL_GC), lambda b, g: (b, 0, g)),)
        + tuple(s[1] for s in cast_specs),
        scratch_shapes=[pltpu.VMEM((lb + 16 * GRID_W, POOL_GC), F32), pltpu.VMEM((lb, POOL_GC), F32)],
        compiler_params=_cparams(("arbitrary", "arbitrary")),
        name="pool",
    )(xc3, xl3, rsb3, inv, nw, sc, sh, ga, ps, pb, pw_bf16, *[cs.w for cs in casts])


def _mlp_kernel(x_ref, nw_ref, sc_ref, sh_ref, g_ref, w1_ref, w2_ref, fw_ref, *rest, final, fc, n_ctx,
                cast_nblk):
    n_cast = len(cast_nblk)
    cast_in, rest = rest[:n_cast], rest[n_cast:]
    outs, (h_ref, acc_ref) = rest[:2 + n_cast], rest[2 + n_cast:]
    if final:
        yc_ref, yl_ref = outs[:2]
    else:
        o_ref, rs_ref = outs[:2]
    i = pl.program_id(0)
    j = pl.program_id(1)
    nj = pl.num_programs(1)

    _do_casts(cast_in, outs[2:], cast_nblk, i * nj + j)

    def ffn(h):
        out = None
        for c0 in range(0, w1_ref.shape[1], fc):
            a = jnp.dot(h, w1_ref[:, c0:c0 + fc], preferred_element_type=F32)
            a = jnp.maximum(a, 0.0)
            d = jnp.dot((a * a).astype(BF16), w2_ref[c0:c0 + fc, :], preferred_element_type=F32)
            out = d if out is None else out + d
        return out

    @pl.when(j == 0)
    def _():
        h = _norm_mod(x_ref[...], nw_ref[...], sc_ref[0], sh_ref[0]).astype(BF16)
        h_ref[...] = h
        acc_ref[...] = ffn(h)

    @pl.when((j > 0) & (j < nj - 1))
    def _():
        acc_ref[...] += ffn(h_ref[...])

    @pl.when(j == nj - 1)
    def _():
        y = x_ref[...] + g_ref[0] * (acc_ref[...] + ffn(h_ref[...]))
        if not final:
            o_ref[...] = y
            rs_ref[...] = _rsqrt_ms_lanes(y)
            return
        rs = lax.rsqrt(jnp.mean(y * y, axis=-1, keepdims=True) + NORM_EPS)
        y = (y * rs) * fw_ref[...]

        @pl.when(i < n_ctx)
        def _():
            yc_ref[...] = y

        @pl.when(i >= n_ctx)
        def _():
            yl_ref[...] = y


def _mlp(tk, x2d, nw, sc, sh, g, w1, w2, fw, tf, final, casts=()):
    nj = D_FF // tf
    assert nj >= 2
    cast_specs = [cs.specs(lambda i, j: i * nj + j) for cs in casts]
    vec = pl.BlockSpec((1, D_MODEL), lambda i, j: (0, 0))
    mod = pl.BlockSpec((1, 1, D_MODEL), lambda i, j: (tk.mod_row(i), 0, 0))
    if final:
        out_shape = (jax.ShapeDtypeStruct((tk.nc, D_MODEL), F32),
                     jax.ShapeDtypeStruct((tk.t - tk.nc, D_MODEL), F32))
        out_specs = (pl.BlockSpec((tk.tm, D_MODEL), lambda i, j: (tk.ctx_tile(i), 0)),
                     pl.BlockSpec((tk.tm, D_MODEL), lambda i, j: (tk.lat_tile(i), 0)))
    else:
        out_shape = (jax.ShapeDtypeStruct((tk.t, D_MODEL), F32), jax.ShapeDtypeStruct((tk.t, LANES), F32))
        out_specs = (pl.BlockSpec((tk.tm, D_MODEL), lambda i, j: (i, 0)),
                     pl.BlockSpec((tk.tm, LANES), lambda i, j: (i, 0)))
    return pl.pallas_call(
        functools.partial(_mlp_kernel, final=final, fc=min(tf, 1024), n_ctx=tk.n_ctx,
                          cast_nblk=tuple(cs.nblk for cs in casts)),
        out_shape=out_shape + tuple(cs.out_shape for cs in casts),
        grid=(tk.nt, nj),
        in_specs=[
            pl.BlockSpec((tk.tm, D_MODEL), lambda i, j: (i, 0)),
            vec, mod, mod, mod,
            pl.BlockSpec((D_MODEL, tf), lambda i, j: (0, j)),
            pl.BlockSpec((tf, D_MODEL), lambda i, j: (j, 0)),
            vec,
        ] + [s[0] for s in cast_specs],
        out_specs=out_specs + tuple(s[1] for s in cast_specs),
        scratch_shapes=[pltpu.VMEM((tk.tm, D_MODEL), BF16), pltpu.VMEM((tk.tm, D_MODEL), F32)],
        compiler_params=_cparams(("arbitrary", "arbitrary")),
        name="mlp",
    )(x2d, nw, sc, sh, g, w1, w2, fw, *[cs.w for cs in casts])


def _inproj_kernel(x_ref, nw_ref, sc_ref, sh_ref, w_ref, cos_ref, sin_ref, *rest, tc, n_ctx, cast_nblk):
    n_cast = len(cast_nblk)
    cast_in, o_ref, cast_out, h_ref = (rest[:n_cast], rest[n_cast], rest[n_cast + 1:2 * n_cast + 1],
                                       rest[2 * n_cast + 1])
    i = pl.program_id(0)
    j = pl.program_id(1)
    nchunk = o_ref.shape[1] // tc
    _do_casts(cast_in, cast_out, cast_nblk, i * pl.num_programs(1) + j)

    def qk_step(rope):
        h = _norm_mod(x_ref[...], nw_ref[...], sc_ref[0], sh_ref[0]).astype(BF16)
        h_ref[...] = h
        for c in range(nchunk):
            p = jnp.dot(h, w_ref[:, c * tc:(c + 1) * tc], preferred_element_type=F32)
            if c * tc >= RET_HK:
                p = p * (RET_DK ** -0.5)
            if not rope:
                o_ref[:, c * tc:(c + 1) * tc] = p.astype(BF16)
                continue
            for t in range(tc // LANES):
                pt = p[:, t * LANES:(t + 1) * LANES]
                c0 = (t % 2) * LANES
                rot = pltpu.roll(pt, LANES // 2, axis=1)
                pt = pt * cos_ref[:, c0:c0 + LANES] + rot * sin_ref[:, c0:c0 + LANES]
                o_ref[:, c * tc + t * LANES:c * tc + (t + 1) * LANES] = pt.astype(BF16)

    pl.when((j == 0) & (i < n_ctx))(lambda: qk_step(False))
    pl.when((j == 0) & (i >= n_ctx))(lambda: qk_step(True))

    @pl.when(j > 0)
    def _():
        h = h_ref[...]
        scale = jnp.where(j == 2, 0.5, 1.0).astype(F32)
        for c in range(nchunk):
            p = jnp.dot(h, w_ref[:, c * tc:(c + 1) * tc], preferred_element_type=F32)
            o_ref[:, c * tc:(c + 1) * tc] = (p * scale).astype(BF16)


def _inproj(tk, x2d, nw, sc, sh, w_in, cos, sin, casts=()):
    n_tab = cos.shape[0] // tk.tm
    tn = 2 * RET_HK
    nj = RET_IN // tn
    assert RET_HV == tn and RET_IN == 3 * tn
    cast_specs = [cs.specs(lambda i, j: i * nj + j) for cs in casts]
    vec = pl.BlockSpec((1, D_MODEL), lambda i, j: (0, 0))
    mod = pl.BlockSpec((1, 1, D_MODEL), lambda i, j: (tk.mod_row(i), 0, 0))
    tab = pl.BlockSpec((tk.tm, RET_DK), lambda i, j: (tk.lat_tile(i) % n_tab, 0))
    return pl.pallas_call(
        functools.partial(_inproj_kernel, tc=512, n_ctx=tk.n_ctx,
                          cast_nblk=tuple(cs.nblk for cs in casts)),
        out_shape=(jax.ShapeDtypeStruct((tk.t, RET_IN), BF16),) + tuple(cs.out_shape for cs in casts),
        grid=(tk.nt, nj),
        in_specs=[
            pl.BlockSpec((tk.tm, D_MODEL), lambda i, j: (i, 0)),
            vec, mod, mod,
            pl.BlockSpec((D_MODEL, tn), lambda i, j: (0, j)),
            tab, tab,
        ] + [s[0] for s in cast_specs],
        out_specs=(pl.BlockSpec((tk.tm, tn), lambda i, j: (i, j)),) + tuple(s[1] for s in cast_specs),
        scratch_shapes=[pltpu.VMEM((tk.tm, D_MODEL), BF16)],
        compiler_params=_cparams(("arbitrary", "arbitrary")),
        name="inproj",
    )(x2d, nw, sc, sh, w_in, cos, sin, *[cs.w for cs in casts])


def _log_gamma(decay):
    return jnp.log1p(-jnp.exp2(-jnp.full((1, 1), decay, F32)))


def _dot_tn(a, b):
    return lax.dot_general(a, b, (((0,), (0,)), ((), ())), preferred_element_type=F32)


def _dot_nt(a, b):
    return lax.dot_general(a, b, (((1,), (1,)), ((), ())), preferred_element_type=F32)


def _ret_kernel(dec_ref, q_ref, k_ref, v_ref, g_ref, gnw_ref, *rest, heads, has_state, emit_states,
                cast_nblk):
    n_cast = len(cast_nblk)
    if has_state:
        s0_ref, rest = rest[0], rest[1:]
    if emit_states:
        (kp_ref, vp_ref, decall_ref), rest = rest[:3], rest[3:]
    cast_in, o_ref, rest = rest[:n_cast], rest[n_cast], rest[n_cast + 1:]
    if emit_states:
        st_ref, rest = rest[0], rest[1:]
    cast_out, rest = rest[:n_cast], rest[n_cast:]
    if has_state:
        oacc_ref, sf_ref, sb_ref = rest
    _do_casts(cast_in, cast_out, cast_nblk, pl.program_id(0) * pl.num_programs(1) + pl.program_id(1))
    c = SLAB
    nchunks = q_ref.shape[1] // c
    row_k = lax.broadcasted_iota(jnp.int32, (c, RET_DK), 0).astype(F32)
    i_f = lax.broadcasted_iota(jnp.int32, (c, c), 0).astype(F32)
    j_f = lax.broadcasted_iota(jnp.int32, (c, c), 1).astype(F32)
    dif = i_f - j_f

    for hh in range(heads):
        head = pl.program_id(1) * heads + hh
        lgf = _log_gamma(dec_ref[0, head])
        lgb = _log_gamma(dec_ref[1, head])
        dmat = jnp.exp(lgf * jnp.maximum(dif, 0.0) + lgb * jnp.maximum(-dif, 0.0))
        gnw = gnw_ref[head]
        ks = slice(hh * RET_DK, (hh + 1) * RET_DK)
        vs = slice(hh * RET_DV, (hh + 1) * RET_DV)

        def intra(r0):
            qc = q_ref[0, pl.ds(r0, c), ks]
            kc = k_ref[0, pl.ds(r0, c), ks]
            vc = v_ref[0, pl.ds(r0, c), vs]
            p = (_dot_nt(qc, kc) * dmat).astype(BF16)
            return qc, kc, vc, jnp.dot(p, vc, preferred_element_type=F32)

        def finalize(r0, o):
            mu = jnp.mean(o, axis=-1, keepdims=True)
            oc = o - mu
            var = jnp.mean(oc * oc, axis=-1, keepdims=True)
            y = oc * lax.rsqrt(var + GN_EPS) * gnw
            hg = g_ref[0, pl.ds(r0, c), vs].astype(F32)
            o_ref[0, pl.ds(r0, c), vs] = ((hg + hg * jnp.tanh(hg)) * y).astype(BF16)

        if emit_states:
            l = nchunks * c
            row_l = lax.broadcasted_iota(jnp.int32, (l, RET_DK), 0).astype(F32)
            for jj, (kk_ref, vv_ref) in enumerate(((kp_ref, vp_ref), (k_ref, v_ref))):
                lsf = _log_gamma(decall_ref[jj, 0, head])
                lsb = _log_gamma(decall_ref[jj, 1, head])
                kk = kk_ref[0, :, ks].astype(F32)
                vv = vv_ref[0, :, vs]
                st_ref[0, jj, 0, hh] = _dot_tn((kk * jnp.exp(lsf * (l - 1.0 - row_l))).astype(BF16), vv)
                st_ref[0, jj, 1, hh] = _dot_tn((kk * jnp.exp(lsb * row_l)).astype(BF16), vv)

        if not has_state:
            for ci in range(nchunks):
                _, _, _, o = intra(ci * c)
                finalize(ci * c, o)
            continue

        xi_f = jnp.exp(lgf * (row_k + 1.0))
        xi_b = jnp.exp(lgb * (c - row_k))
        zeta_f = jnp.exp(lgf * (c - 1.0 - row_k))
        zeta_b = jnp.exp(lgb * row_k)
        gc_f = jnp.exp(lgf * c)
        gc_b = jnp.exp(lgb * c)
        sf_ref[...] = s0_ref[0, 0, 0, 0]
        sb_ref[...] = s0_ref[0, 0, 1, 0]

        def fwd_part(r0):
            qc, kc, vc, o = intra(r0)
            qx = (qc.astype(F32) * xi_f).astype(BF16)
            o = o + jnp.dot(qx, sf_ref[...].astype(BF16), preferred_element_type=F32)
            kz = (kc.astype(F32) * zeta_f).astype(BF16)
            sf_ref[...] = gc_f * sf_ref[...] + _dot_tn(kz, vc)
            return o

        def bwd_part(r0):
            qc = q_ref[0, pl.ds(r0, c), ks]
            kc = k_ref[0, pl.ds(r0, c), ks]
            vc = v_ref[0, pl.ds(r0, c), vs]
            qx = (qc.astype(F32) * xi_b).astype(BF16)
            o = jnp.dot(qx, sb_ref[...].astype(BF16), preferred_element_type=F32)
            kz = (kc.astype(F32) * zeta_b).astype(BF16)
            sb_ref[...] = gc_b * sb_ref[...] + _dot_tn(kz, vc)
            return o

        def first_half(t, carry):
            rf = pl.multiple_of(t * c, c)
            rb = pl.multiple_of((nchunks - 1 - t) * c, c)
            oacc_ref[pl.ds(rf, c), :] = fwd_part(rf)
            oacc_ref[pl.ds(rb, c), :] = bwd_part(rb)
            return carry

        def second_half(t, carry):
            rf = pl.multiple_of(t * c, c)
            rb = pl.multiple_of((nchunks - 1 - t) * c, c)
            finalize(rf, oacc_ref[pl.ds(rf, c), :] + fwd_part(rf))
            finalize(rb, oacc_ref[pl.ds(rb, c), :] + bwd_part(rb))
            return carry

        assert nchunks % 2 == 0
        lax.fori_loop(0, nchunks // 2, first_half, 0)
        lax.fori_loop(nchunks // 2, nchunks, second_half, 0)


def _retcore(qkvg, b_off, nb, decay, gnw, state, layer_j, heads, casts=(), states_of=None):
    _, l, _ = qkvg.shape
    has_state = state is not None
    nh = RET_HEADS // heads
    q_off = 0
    k_off = RET_HK // (RET_DK * heads)
    v_off = 2 * RET_HK // (RET_DV * heads)
    g_off = (2 * RET_HK + RET_HV) // (RET_DV * heads)
    in_specs = [
        pl.BlockSpec(memory_space=pltpu.SMEM),
        pl.BlockSpec((1, l, RET_DK * heads), lambda i, h: (b_off + i, 0, q_off + h)),
        pl.BlockSpec((1, l, RET_DK * heads), lambda i, h: (b_off + i, 0, k_off + h)),
        pl.BlockSpec((1, l, RET_DV * heads), lambda i, h: (b_off + i, 0, v_off + h)),
        pl.BlockSpec((1, l, RET_DV * heads), lambda i, h: (b_off + i, 0, g_off + h)),
        pl.BlockSpec((RET_HEADS, 1, RET_DV), lambda i, h: (0, 0, 0)),
    ]
    args = [decay, qkvg, qkvg, qkvg, qkvg, gnw.reshape(RET_HEADS, 1, RET_DV)]
    scratch = []
    if has_state:
        assert heads == 1
        in_specs.append(pl.BlockSpec((1, 1, 2, 1, RET_DK, RET_DV),
                                     lambda i, h: (i, layer_j, 0, h, 0, 0)))
        args.append(state)
        scratch = [pltpu.VMEM((l, RET_DV), F32), pltpu.VMEM((RET_DK, RET_DV), F32),
                   pltpu.VMEM((RET_DK, RET_DV), F32)]
    out_shape = [jax.ShapeDtypeStruct((nb, l, RET_HV), BF16)]
    out_specs = [pl.BlockSpec((1, l, RET_DV * heads), lambda i, h: (i, 0, h))]
    if states_of is not None:
        qkvg_prev, decay_all = states_of
        assert heads == RET_HEADS and not has_state and decay_all.shape[0] == 2
        in_specs += [pl.BlockSpec((1, l, RET_HK), lambda i, h: (b_off + i, 0, 1)),
                     pl.BlockSpec((1, l, RET_HV), lambda i, h: (b_off + i, 0, 1)),
                     pl.BlockSpec(memory_space=pltpu.SMEM)]
        args += [qkvg_prev, qkvg_prev, decay_all]
        out_shape.append(jax.ShapeDtypeStruct((nb, 2, 2, RET_HEADS, RET_DK, RET_DV), F32))
        out_specs.append(pl.BlockSpec((1, 2, 2, RET_HEADS, RET_DK, RET_DV),
                                      lambda i, h: (i, 0, 0, 0, 0, 0)))
    cast_specs = [cs.specs(lambda i, h: i * nh + h) for cs in casts]
    return pl.pallas_call(
        functools.partial(_ret_kernel, heads=heads, has_state=has_state,
                          emit_states=states_of is not None,
                          cast_nblk=tuple(cs.nblk for cs in casts)),
        out_shape=tuple(out_shape) + tuple(cs.out_shape for cs in casts),
        grid=(nb, nh),
        in_specs=in_specs + [s[0] for s in cast_specs],
        out_specs=tuple(out_specs) + tuple(s[1] for s in cast_specs),
        scratch_shapes=scratch,
        compiler_params=_cparams(("arbitrary", "arbitrary")),
        name="retcore_state" if has_state else "retcore",
    )(*args, *[cs.w for cs in casts])


def _outproj_kernel(x_ref, goc_ref, gol_ref, w_ref, g_ref, o_ref, *, n_ctx):
    i = pl.program_id(0)

    def emit(go_ref):
        mix = jnp.dot(go_ref[...], w_ref[...], preferred_element_type=F32)
        o_ref[...] = x_ref[...] + g_ref[0] * mix

    pl.when(i < n_ctx)(lambda: emit(goc_ref))
    pl.when(i >= n_ctx)(lambda: emit(gol_ref))


def _outproj(tk, x2d, go_c, go_l, w_out, ga):
    return pl.pallas_call(
        functools.partial(_outproj_kernel, n_ctx=tk.n_ctx),
        out_shape=jax.ShapeDtypeStruct((tk.t, D_MODEL), F32),
        grid=(tk.nt,),
        in_specs=[
            pl.BlockSpec((tk.tm, D_MODEL), lambda i: (i, 0)),
            pl.BlockSpec((tk.tm, RET_HV), lambda i: (tk.ctx_tile(i), 0)),
            pl.BlockSpec((tk.tm, RET_HV), lambda i: (tk.lat_tile(i), 0)),
            pl.BlockSpec((RET_HV, D_MODEL), lambda i: (0, 0)),
            pl.BlockSpec((1, 1, D_MODEL), lambda i: (tk.mod_row(i), 0, 0)),
        ],
        out_specs=pl.BlockSpec((tk.tm, D_MODEL), lambda i: (i, 0)),
        compiler_params=_cparams(("arbitrary",)),
        name="outproj",
    )(x2d, go_c, go_l, w_out, ga)


def _rope_tables(length):
    quarter = RET_DK // 4
    half = RET_DK // 2
    rows = length // GRID_W
    freqs = ROPE_BASE ** (-jnp.arange(quarter, dtype=F32) / quarter)
    sign = jnp.concatenate([-jnp.ones((quarter,), F32), jnp.ones((quarter,), F32)])

    def tabs(npos):
        ang = jnp.arange(npos, dtype=F32)[:, None] * freqs[None, :]
        return (jnp.concatenate([jnp.cos(ang), jnp.cos(ang)], axis=-1),
                jnp.concatenate([jnp.sin(ang), jnp.sin(ang)], axis=-1) * sign)

    def by_row(tab):
        return jnp.broadcast_to(tab[:, None, :], (rows, GRID_W, half)).reshape(length, half)

    def by_col(tab):
        return jnp.broadcast_to(tab[None, :, :], (rows, GRID_W, half)).reshape(length, half)

    cos_r, sin_r = tabs(rows)
    cos_c, sin_c = tabs(GRID_W)
    return (jnp.concatenate([by_row(cos_r), by_col(cos_c)], axis=-1),
            jnp.concatenate([by_row(sin_r), by_col(sin_c)], axis=-1))


def kernel(x_prompt, x_sample, state_ret, c, c_ctx, w_ada, b_ada, norm_mix_w, norm_mlp_w, pool_w,
           pool_b, pool_scale, ret_w_in, ret_decay, ret_gn_w, ret_w_out, mlp_w1, mlp_w2,
           final_norm_w):
    nb_ctx, seq, _ = x_prompt.shape
    nb_lat, lat, _ = x_sample.shape
    tk = _Tokens(nb_ctx * seq, lat, nb_lat, TM)
    tk_mlp = _Tokens(nb_ctx * seq, lat, nb_lat, TM_MLP)
    assert tk.nc == lat and seq == SLAB

    cond8 = jnp.concatenate([c_ctx[None, :], c, jnp.zeros((8 - 1 - nb_lat, D_MODEL), F32)], axis=0)
    mods = _mods(cond8, w_ada, b_ada)
    mods = mods.reshape(DEPTH, 8, N_MOD, D_MODEL).transpose(0, 2, 1, 3)
    mods = mods[:, :, :1 + nb_lat, None, :]

    cos, sin = _rope_tables(lat)
    inv = jnp.stack([_inv_count_table(lat, False), _inv_count_table(lat, True)])
    pw = pool_w.astype(BF16)
    fw = final_norm_w.reshape(1, D_MODEL)

    def mlp_casts(layer):
        return (_Cast(mlp_w1, layer, 16), _Cast(mlp_w2, layer, 16))

    rsb = _rms(tk, x_prompt.reshape(tk.nc, D_MODEL), x_sample.reshape(nb_lat * lat, D_MODEL))
    x = None
    qkvgs = []
    mlp_w = {}
    for i in range(DEPTH):
        j = i // 2
        sh_a, sc_a, g_a, sh_m, sc_m, g_m = (mods[i, k] for k in range(N_MOD))
        nw_a = norm_mix_w[i].reshape(1, D_MODEL)
        if i % 2 == 0:
            casts = () if i in mlp_w else (_Cast(mlp_w1, i, 8), _Cast(mlp_w2, i, 8))
            if x is None:
                xc3, xl3, xl_off = x_prompt.reshape(1, tk.nc, D_MODEL), x_sample, 1
            else:
                xc3 = xl3 = x.reshape(1 + nb_lat, lat, D_MODEL)
                xl_off = 0
            x, *conv = _pool(xc3, xl3, xl_off, rsb.reshape(1 + nb_lat, lat, LANES),
                             inv, nw_a, sc_a, sh_a, g_a, pool_scale[j].reshape(1, D_MODEL),
                             pool_b[j].reshape(1, D_MODEL), pw, j, casts=casts)
            if conv:
                mlp_w[i] = conv
            x = x.reshape(tk.t, D_MODEL)
            host_casts = (_Cast(ret_w_in, j, 16),)
        else:
            later = [l for l in (i, i + 1) if l < DEPTH]
            qkvg, *conv = _inproj(tk, x, nw_a, sc_a, sh_a, w_in, cos, sin,
                                  casts=sum((mlp_casts(l) for l in later), ()))
            for n, l in enumerate(later):
                mlp_w[l] = conv[2 * n:2 * n + 2]
            qkvgs.append(qkvg)
            states_of = None
            if len(qkvgs) == 2:
                states_of = (qkvgs[0].reshape(tk.t // seq, seq, RET_IN), ret_decay)
            go_c, *new_state = _retcore(qkvg.reshape(tk.t // seq, seq, RET_IN), 0, nb_ctx, ret_decay[j],
                                        ret_gn_w[j], None, j, RET_HEADS, states_of=states_of)
            go_l, w_out = _retcore(qkvg.reshape(tk.t // lat, lat, RET_IN), tk.nc // lat, nb_lat,
                                   ret_decay[j], ret_gn_w[j], state_ret, j, 1,
                                   casts=(_Cast(ret_w_out, j, 8),))
            x = _outproj(tk, x, go_c.reshape(tk.nc, RET_HV), go_l.reshape(nb_lat * lat, RET_HV),
                         w_out, g_a)
            host_casts = ()
        w1, w2 = mlp_w[i]
        x, rsb, *conv = _mlp(tk_mlp, x, norm_mlp_w[i].reshape(1, D_MODEL), sc_m, sh_m, g_m, w1, w2, fw,
                             2048, i == DEPTH - 1, casts=host_casts)
        if conv:
            w_in, = conv
    y_prompt, y_sample = x, rsb
    return (y_prompt.reshape(x_prompt.shape), y_sample.reshape(x_sample.shape), new_state[0])
```

```python
import functools

import jax
import jax.numpy as jnp
from jax import lax
from jax.experimental import pallas as pl
from jax.experimental.pallas import tpu as pltpu

F32 = jnp.float32
BF16 = jnp.bfloat16

D_MODEL = 1024
DEPTH = 4
GRID_W = 64
POOL_WINDOWS = (2, 4, 8, 16)
POOL_GC = 256
RET_HEADS = 4
RET_DK = 256
RET_DV = 512
RET_HK = RET_HEADS * RET_DK
RET_HV = RET_HEADS * RET_DV
RET_IN = 2 * RET_HK + 2 * RET_HV
D_FF = 4 * D_MODEL
ROPE_BASE = 10000.0
NORM_EPS = 1e-6
GN_EPS = 1e-5
N_MOD = 6

LANES = 128
SLAB = 256
TM = 1024
TM_MLP = 1024
VMEM_LIMIT = 56 * 1024 * 1024


def _cparams(sem):
    return pltpu.CompilerParams(dimension_semantics=sem, vmem_limit_bytes=VMEM_LIMIT)


def _norm_mod(x, nw, sc, sh):
    ms = jnp.mean(x * x, axis=-1, keepdims=True)
    return (x * lax.rsqrt(ms + NORM_EPS)) * nw * (1.0 + sc) + sh


def _rsqrt_ms_lanes(x):
    rs = lax.rsqrt(jnp.mean(x * x, axis=-1, keepdims=True) + NORM_EPS)
    return jnp.broadcast_to(rs, (x.shape[0], LANES))


class _Tokens:
    def __init__(self, nc, lat, nlat, tm):
        assert nc % tm == 0 and lat % tm == 0
        self.nc, self.lat, self.nlat, self.tm = nc, lat, nlat, tm
        self.t = nc + lat * nlat
        self.n_ctx = nc // tm
        self.nt = self.t // tm

    def mod_row(self, i):
        return jnp.where(i < self.n_ctx, 0, 1 + (i - self.n_ctx) // (self.lat // self.tm))

    def ctx_tile(self, i):
        return jnp.minimum(i, self.n_ctx - 1)

    def lat_tile(self, i):
        return jnp.maximum(i - self.n_ctx, 0)


class _Cast:
    def __init__(self, w, layer, nblk):
        _, r, c = w.shape
        assert r % (16 * nblk) == 0
        self.w, self.layer, self.nblk, self.rows, self.cols = w, layer, nblk, r // nblk, c
        self.out_shape = jax.ShapeDtypeStruct((r, c), BF16)

    def specs(self, step):
        def blk(*ids):
            return jnp.minimum(step(*ids), self.nblk - 1)
        return (pl.BlockSpec((None, self.rows, self.cols), lambda *ids: (self.layer, blk(*ids), 0)),
                pl.BlockSpec((self.rows, self.cols), lambda *ids: (blk(*ids), 0)))


def _do_casts(src_refs, dst_refs, nblks, step):
    for src_ref, dst_ref, nblk in zip(src_refs, dst_refs, nblks):
        @pl.when(step < nblk)
        def _(src_ref=src_ref, dst_ref=dst_ref):
            dst_ref[...] = src_ref[...].astype(BF16)


def _mods_kernel(c_ref, w_ref, b_ref, o_ref):
    c = c_ref[...]
    s = c * jax.nn.sigmoid(c)
    w = w_ref[0].astype(BF16)
    o_ref[0] = jnp.dot(s.astype(BF16), w, preferred_element_type=F32) + b_ref[0]


def _mods(cond8, w_ada, b_ada):
    tn = 1024
    n = N_MOD * D_MODEL
    return pl.pallas_call(
        _mods_kernel,
        out_shape=jax.ShapeDtypeStruct((DEPTH, 8, n), F32),
        grid=(DEPTH, n // tn),
        in_specs=[
            pl.BlockSpec((8, D_MODEL), lambda l, j: (0, 0)),
            pl.BlockSpec((1, D_MODEL, tn), lambda l, j: (l, 0, j)),
            pl.BlockSpec((1, 1, tn), lambda l, j: (l, 0, j)),
        ],
        out_specs=pl.BlockSpec((1, 8, tn), lambda l, j: (l, 0, j)),
        compiler_params=_cparams(("parallel", "parallel")),
        name="mods",
    )(cond8, w_ada, b_ada.reshape(DEPTH, 1, n))


def _rms_kernel(xc_ref, xl_ref, rs_ref, *, n_ctx):
    i = pl.program_id(0)

    @pl.when(i < n_ctx)
    def _():
        rs_ref[...] = _rsqrt_ms_lanes(xc_ref[...])

    @pl.when(i >= n_ctx)
    def _():
        rs_ref[...] = _rsqrt_ms_lanes(xl_ref[...])


def _rms(tk, xc, xl):
    return pl.pallas_call(
        functools.partial(_rms_kernel, n_ctx=tk.n_ctx),
        out_shape=jax.ShapeDtypeStruct((tk.t, LANES), F32),
        grid=(tk.nt,),
        in_specs=[pl.BlockSpec((tk.tm, D_MODEL), lambda i: (tk.ctx_tile(i), 0)),
                  pl.BlockSpec((tk.tm, D_MODEL), lambda i: (tk.lat_tile(i), 0))],
        out_specs=pl.BlockSpec((tk.tm, LANES), lambda i: (i, 0)),
        compiler_params=_cparams(("arbitrary",)),
        name="rms",
    )(xc, xl)


def _window_count(pos, w, length):
    half = w // 2
    return jnp.minimum(pos - half + w, length) - jnp.maximum(pos - half, 0)


def _inv_count_table(lb, grid):
    t = jnp.arange(lb)
    rows = []
    for w in POOL_WINDOWS:
        if grid:
            cnt = _window_count(t // GRID_W, w, lb // GRID_W) * _window_count(t % GRID_W, w, GRID_W)
        else:
            cnt = _window_count(t % SLAB, w, SLAB)
        rows.append(1.0 / cnt.astype(F32))
    return jnp.broadcast_to(jnp.stack(rows)[:, :, None], (len(POOL_WINDOWS), lb, LANES))


def _pool_group(w, grid, x_ref, rs_ref, inv_ref, nw_ref, sc_ref, sh_ref, ga_ref, ps_ref, pb_ref,
                pw_ref, o_ref, colp_ref, hbuf_ref):
    half = w // 2
    nslab = x_ref.shape[1] // SLAB
    t_i = lax.broadcasted_iota(jnp.int32, (SLAB, SLAB), 0)
    s_i = lax.broadcasted_iota(jnp.int32, (SLAB, SLAB), 1)
    diff = s_i - t_i
    band = (diff >= -half) & (diff <= w - half - 1)
    if grid:
        band = band & ((s_i >> 6) == (t_i >> 6))
    sm = jnp.where(band, 1.0, 0.0).astype(BF16)

    def inv_cnt(r0):
        v = inv_ref[pl.ds(r0, SLAB), :]
        return jnp.concatenate([v, v], axis=1)

    a = nw_ref[...] * (1.0 + sc_ref[0])
    sh = sh_ref[0]
    scale_out = ga_ref[0] * ps_ref[...]
    pw = pw_ref[...]
    pb = pb_ref[...]

    def slab_h(r0):
        x = x_ref[0, pl.ds(r0, SLAB), :]
        rs = rs_ref[0, pl.ds(r0, SLAB), :]
        rs2 = jnp.concatenate([rs, rs], axis=1)
        return x, x * rs2 * a + sh

    def window_sum(h):
        return jnp.dot(sm, h.astype(BF16), preferred_element_type=F32)

    def finish(r0, x, h, m):
        d = (m - h).astype(BF16)
        mix = jnp.dot(d, pw, preferred_element_type=F32) + pb
        o_ref[0, pl.ds(r0, SLAB), :] = x + scale_out * mix

    pad = 8 * GRID_W
    if grid:
        zeros = jnp.zeros((pad, POOL_GC), F32)
        colp_ref[pl.ds(0, pad), :] = zeros
        colp_ref[pl.ds(pad + nslab * SLAB, pad), :] = zeros
    row_offsets = [(j - half) * GRID_W for j in range(w)] if grid else [0]

    def col_body(s, carry):
        r0 = pl.multiple_of(s * SLAB, SLAB)
        _, h = slab_h(r0)
        hbuf_ref[pl.ds(r0, SLAB), :] = h
        colp_ref[pl.ds(pad + r0, SLAB), :] = window_sum(h)
        return carry

    lax.fori_loop(0, nslab, col_body, 0, unroll=16)

    def row_body(s, carry):
        r0 = pl.multiple_of(s * SLAB, SLAB)
        acc = colp_ref[pl.ds(pad + r0 + row_offsets[0], SLAB), :]
        for off in row_offsets[1:]:
            acc = acc + colp_ref[pl.ds(pad + r0 + off, SLAB), :]
        finish(r0, x_ref[0, pl.ds(r0, SLAB), :], hbuf_ref[pl.ds(r0, SLAB), :], acc * inv_cnt(r0))
        return carry

    lax.fori_loop(0, nslab, row_body, 0, unroll=8)


def _pool_kernel(xc_ref, xl_ref, rs_ref, inv_ref, nw_ref, sc_ref, sh_ref, ga_ref, ps_ref, pb_ref, pw_ref,
                 *rest, cast_nblk):
    n_cast = len(cast_nblk)
    cast_in, o_ref, cast_out = rest[:n_cast], rest[n_cast], rest[n_cast + 1:2 * n_cast + 1]
    colp_ref, hbuf_ref = rest[2 * n_cast + 1:]
    b = pl.program_id(0)
    g = pl.program_id(1)
    _do_casts(cast_in, cast_out, cast_nblk, b * pl.num_programs(1) + g)
    for gi, w in enumerate(POOL_WINDOWS):
        for grid in (False, True):
            @pl.when((g == gi) & ((b > 0) if grid else (b == 0)))
            def _(w=w, grid=grid):
                _pool_group(w, grid, xl_ref if grid else xc_ref, rs_ref, inv_ref, nw_ref, sc_ref, sh_ref,
                            ga_ref, ps_ref, pb_ref, pw_ref, o_ref, colp_ref, hbuf_ref)


def _pool(xc3, xl3, xl_off, rsb3, inv, nw, sc, sh, ga, ps, pb, pw_bf16, layer_j, casts=()):
    nb, lb, _ = rsb3.shape
    ng = len(POOL_WINDOWS)
    cast_specs = [cs.specs(lambda b, g: b * ng + g) for cs in casts]
    vec = pl.BlockSpec((1, POOL_GC), lambda b, g: (0, g))
    mod = pl.BlockSpec((1, 1, POOL_GC), lambda b, g: (b, 0, g))
    return pl.pallas_call(
        functools.partial(_pool_kernel, cast_nblk=tuple(cs.nblk for cs in casts)),
        out_shape=(jax.ShapeDtypeStruct((nb, lb, D_MODEL), F32),) + tuple(cs.out_shape for cs in casts),
        grid=(nb, ng),
        in_specs=[
            pl.BlockSpec((1, lb, POOL_GC), lambda b, g: (0, 0, jnp.where(b == 0, g, ng - 1))),
            pl.BlockSpec((1, lb, POOL_GC),
                         lambda b, g: (jnp.maximum(b, 1) - xl_off, 0, jnp.where(b == 0, 0, g))),
            pl.BlockSpec((1, lb, LANES), lambda b, g: (b, 0, 0)),
            pl.BlockSpec((None, None, lb, LANES), lambda b, g: (jnp.minimum(b, 1), g, 0, 0)),
            vec, mod, mod, mod, vec, vec,
            pl.BlockSpec((None, None, POOL_GC, POOL_GC), lambda b, g: (layer_j, g, 0, 0)),
        ] + [s[0] for s in cast_specs],
        out_specs=(pl.BlockSpec((1, lb, POOL_GC), lambda b, g: (b, 0, g)),)
        + tuple(s[1] for s in cast_specs),
        scratch_shapes=[pltpu.VMEM((lb + 16 * GRID_W, POOL_GC), F32), pltpu.VMEM((lb, POOL_GC), F32)],
        compiler_params=_cparams(("arbitrary", "arbitrary")),
        name="pool",
    )(xc3, xl3, rsb3, inv, nw, sc, sh, ga, ps, pb, pw_bf16, *[cs.w for cs in casts])


def _mlp_kernel(x_ref, nw_ref, sc_ref, sh_ref, g_ref, w1_ref, w2_ref, fw_ref, *rest, final, fc, n_ctx,
                cast_nblk):
    n_cast = len(cast_nblk)
    cast_in, rest = rest[:n_cast], rest[n_cast:]
    outs, (h_ref, acc_ref) = rest[:2 + n_cast], rest[2 + n_cast:]
    if final:
        yc_ref, yl_ref = outs[:2]
    else:
        o_ref, rs_ref = outs[:2]
    i = pl.program_id(0)
    j = pl.program_id(1)
    nj = pl.num_programs(1)

    _do_casts(cast_in, outs[2:], cast_nblk, i * nj + j)

    def ffn(h):
        out = None
        for c0 in range(0, w1_ref.shape[1], fc):
            a = jnp.dot(h, w1_ref[:, c0:c0 + fc], preferred_element_type=F32)
            a = jnp.maximum(a, 0.0)
            d = jnp.dot((a * a).astype(BF16), w2_ref[c0:c0 + fc, :], preferred_element_type=F32)
            out = d if out is None else out + d
        return out

    @pl.when(j == 0)
    def _():
        h = _norm_mod(x_ref[...], nw_ref[...], sc_ref[0], sh_ref[0]).astype(BF16)
        h_ref[...] = h
        acc_ref[...] = ffn(h)

    @pl.when((j > 0) & (j < nj - 1))
    def _():
        acc_ref[...] += ffn(h_ref[...])

    @pl.when(j == nj - 1)
    def _():
        y = x_ref[...] + g_ref[0] * (acc_ref[...] + ffn(h_ref[...]))
        if not final:
            o_ref[...] = y
            rs_ref[...] = _rsqrt_ms_lanes(y)
            return
        rs = lax.rsqrt(jnp.mean(y * y, axis=-1, keepdims=True) + NORM_EPS)
        y = (y * rs) * fw_ref[...]

        @pl.when(i < n_ctx)
        def _():
            yc_ref[...] = y

        @pl.when(i >= n_ctx)
        def _():
            yl_ref[...] = y


def _mlp(tk, x2d, nw, sc, sh, g, w1, w2, fw, tf, final, casts=()):
    nj = D_FF // tf
    assert nj >= 2
    cast_specs = [cs.specs(lambda i, j: i * nj + j) for cs in casts]
    vec = pl.BlockSpec((1, D_MODEL), lambda i, j: (0, 0))
    mod = pl.BlockSpec((1, 1, D_MODEL), lambda i, j: (tk.mod_row(i), 0, 0))
    if final:
        out_shape = (jax.ShapeDtypeStruct((tk.nc, D_MODEL), F32),
                     jax.ShapeDtypeStruct((tk.t - tk.nc, D_MODEL), F32))
        out_specs = (pl.BlockSpec((tk.tm, D_MODEL), lambda i, j: (tk.ctx_tile(i), 0)),
                     pl.BlockSpec((tk.tm, D_MODEL), lambda i, j: (tk.lat_tile(i), 0)))
    else:
        out_shape = (jax.ShapeDtypeStruct((tk.t, D_MODEL), F32), jax.ShapeDtypeStruct((tk.t, LANES), F32))
        out_specs = (pl.BlockSpec((tk.tm, D_MODEL), lambda i, j: (i, 0)),
                     pl.BlockSpec((tk.tm, LANES), lambda i, j: (i, 0)))
    return pl.pallas_call(
        functools.partial(_mlp_kernel, final=final, fc=min(tf, 1024), n_ctx=tk.n_ctx,
                          cast_nblk=tuple(cs.nblk for cs in casts)),
        out_shape=out_shape + tuple(cs.out_shape for cs in casts),
        grid=(tk.nt, nj),
        in_specs=[
            pl.BlockSpec((tk.tm, D_MODEL), lambda i, j: (i, 0)),
            vec, mod, mod, mod,
            pl.BlockSpec((D_MODEL, tf), lambda i, j: (0, j)),
            pl.BlockSpec((tf, D_MODEL), lambda i, j: (j, 0)),
            vec,
        ] + [s[0] for s in cast_specs],
        out_specs=out_specs + tuple(s[1] for s in cast_specs),
        scratch_shapes=[pltpu.VMEM((tk.tm, D_MODEL), BF16), pltpu.VMEM((tk.tm, D_MODEL), F32)],
        compiler_params=_cparams(("arbitrary", "arbitrary")),
        name="mlp",
    )(x2d, nw, sc, sh, g, w1, w2, fw, *[cs.w for cs in casts])


def _inproj_kernel(x_ref, nw_ref, sc_ref, sh_ref, w_ref, cos_ref, sin_ref, *rest, tc, n_ctx, cast_nblk):
    n_cast = len(cast_nblk)
    cast_in, o_ref, cast_out, h_ref = (rest[:n_cast], rest[n_cast], rest[n_cast + 1:2 * n_cast + 1],
                                       rest[2 * n_cast + 1])
    i = pl.program_id(0)
    j = pl.program_id(1)
    nchunk = o_ref.shape[1] // tc
    _do_casts(cast_in, cast_out, cast_nblk, i * pl.num_programs(1) + j)

    def qk_step(rope):
        h = _norm_mod(x_ref[...], nw_ref[...], sc_ref[0], sh_ref[0]).astype(BF16)
        h_ref[...] = h
        for c in range(nchunk):
            p = jnp.dot(h, w_ref[:, c * tc:(c + 1) * tc], preferred_element_type=F32)
            if c * tc >= RET_HK:
                p = p * (RET_DK ** -0.5)
            if not rope:
                o_ref[:, c * tc:(c + 1) * tc] = p.astype(BF16)
                continue
            for t in range(tc // LANES):
                pt = p[:, t * LANES:(t + 1) * LANES]
                c0 = (t % 2) * LANES
                rot = pltpu.roll(pt, LANES // 2, axis=1)
                pt = pt * cos_ref[:, c0:c0 + LANES] + rot * sin_ref[:, c0:c0 + LANES]
                o_ref[:, c * tc + t * LANES:c * tc + (t + 1) * LANES] = pt.astype(BF16)

    pl.when((j == 0) & (i < n_ctx))(lambda: qk_step(False))
    pl.when((j == 0) & (i >= n_ctx))(lambda: qk_step(True))

    @pl.when(j > 0)
    def _():
        h = h_ref[...]
        scale = jnp.where(j == 2, 0.5, 1.0).astype(F32)
        for c in range(nchunk):
            p = jnp.dot(h, w_ref[:, c * tc:(c + 1) * tc], preferred_element_type=F32)
            o_ref[:, c * tc:(c + 1) * tc] = (p * scale).astype(BF16)


def _inproj(tk, x2d, nw, sc, sh, w_in, cos, sin, casts=()):
    n_tab = cos.shape[0] // tk.tm
    tn = 2 * RET_HK
    nj = RET_IN // tn
    assert RET_HV == tn and RET_IN == 3 * tn
    cast_specs = [cs.specs(lambda i, j: i * nj + j) for cs in casts]
    vec = pl.BlockSpec((1, D_MODEL), lambda i, j: (0, 0))
    mod = pl.BlockSpec((1, 1, D_MODEL), lambda i, j: (tk.mod_row(i), 0, 0))
    tab = pl.BlockSpec((tk.tm, RET_DK), lambda i, j: (tk.lat_tile(i) % n_tab, 0))
    return pl.pallas_call(
        functools.partial(_inproj_kernel, tc=512, n_ctx=tk.n_ctx,
                          cast_nblk=tuple(cs.nblk for cs in casts)),
        out_shape=(jax.ShapeDtypeStruct((tk.t, RET_IN), BF16),) + tuple(cs.out_shape for cs in casts),
        grid=(tk.nt, nj),
        in_specs=[
            pl.BlockSpec((tk.tm, D_MODEL), lambda i, j: (i, 0)),
            vec, mod, mod,
            pl.BlockSpec((D_MODEL, tn), lambda i, j: (0, j)),
            tab, tab,
        ] + [s[0] for s in cast_specs],
        out_specs=(pl.BlockSpec((tk.tm, tn), lambda i, j: (i, j)),) + tuple(s[1] for s in cast_specs),
        scratch_shapes=[pltpu.VMEM((tk.tm, D_MODEL), BF16)],
        compiler_params=_cparams(("arbitrary", "arbitrary")),
        name="inproj",
    )(x2d, nw, sc, sh, w_in, cos, sin, *[cs.w for cs in casts])


def _log_gamma(decay):
    return jnp.log1p(-jnp.exp2(-jnp.full((1, 1), decay, F32)))


def _dot_tn(a, b):
    return lax.dot_general(a, b, (((0,), (0,)), ((), ())), preferred_element_type=F32)


def _dot_nt(a, b):
    return lax.dot_general(a, b, (((1,), (1,)), ((), ())), preferred_element_type=F32)


def _ret_kernel(dec_ref, q_ref, k_ref, v_ref, g_ref, gnw_ref, *rest, heads, has_state, emit_states,
                cast_nblk):
    n_cast = len(cast_nblk)
    if has_state:
        s0_ref, rest = rest[0], rest[1:]
    if emit_states:
        (kp_ref, vp_ref, decall_ref), rest = rest[:3], rest[3:]
    cast_in, o_ref, rest = rest[:n_cast], rest[n_cast], rest[n_cast + 1:]
    if emit_states:
        st_ref, rest = rest[0], rest[1:]
    cast_out, rest = rest[:n_cast], rest[n_cast:]
    if has_state:
        oacc_ref, sf_ref, sb_ref = rest
    _do_casts(cast_in, cast_out, cast_nblk, pl.program_id(0) * pl.num_programs(1) + pl.program_id(1))
    c = SLAB
    nchunks = q_ref.shape[1] // c
    row_k = lax.broadcasted_iota(jnp.int32, (c, RET_DK), 0).astype(F32)
    i_f = lax.broadcasted_iota(jnp.int32, (c, c), 0).astype(F32)
    j_f = lax.broadcasted_iota(jnp.int32, (c, c), 1).astype(F32)
    dif = i_f - j_f

    for hh in range(heads):
        head = pl.program_id(1) * heads + hh
        lgf = _log_gamma(dec_ref[0, head])
        lgb = _log_gamma(dec_ref[1, head])
        dmat = jnp.exp(lgf * jnp.maximum(dif, 0.0) + lgb * jnp.maximum(-dif, 0.0))
        gnw = gnw_ref[head]
        ks = slice(hh * RET_DK, (hh + 1) * RET_DK)
        vs = slice(hh * RET_DV, (hh + 1) * RET_DV)

        def intra(r0):
            qc = q_ref[0, pl.ds(r0, c), ks]
            kc = k_ref[0, pl.ds(r0, c), ks]
            vc = v_ref[0, pl.ds(r0, c), vs]
            p = (_dot_nt(qc, kc) * dmat).astype(BF16)
            return qc, kc, vc, jnp.dot(p, vc, preferred_element_type=F32)

        def finalize(r0, o):
            mu = jnp.mean(o, axis=-1, keepdims=True)
            oc = o - mu
            var = jnp.mean(oc * oc, axis=-1, keepdims=True)
            y = oc * lax.rsqrt(var + GN_EPS) * gnw
            hg = g_ref[0, pl.ds(r0, c), vs].astype(F32)
            o_ref[0, pl.ds(r0, c), vs] = ((hg + hg * jnp.tanh(hg)) * y).astype(BF16)

        if emit_states:
            l = nchunks * c
            row_l = lax.broadcasted_iota(jnp.int32, (l, RET_DK), 0).astype(F32)
            for jj, (kk_ref, vv_ref) in enumerate(((kp_ref, vp_ref), (k_ref, v_ref))):
                lsf = _log_gamma(decall_ref[jj, 0, head])
                lsb = _log_gamma(decall_ref[jj, 1, head])
                kk = kk_ref[0, :, ks].astype(F32)
                vv = vv_ref[0, :, vs]
                st_ref[0, jj, 0, hh] = _dot_tn((kk * jnp.exp(lsf * (l - 1.0 - row_l))).astype(BF16), vv)
                st_ref[0, jj, 1, hh] = _dot_tn((kk * jnp.exp(lsb * row_l)).astype(BF16), vv)

        if not has_state:
            for ci in range(nchunks):
                _, _, _, o = intra(ci * c)
                finalize(ci * c, o)
            continue

        xi_f = jnp.exp(lgf * (row_k + 1.0))
        xi_b = jnp.exp(lgb * (c - row_k))
        zeta_f = jnp.exp(lgf * (c - 1.0 - row_k))
        zeta_b = jnp.exp(lgb * row_k)
        gc_f = jnp.exp(lgf * c)
        gc_b = jnp.exp(lgb * c)
        sf_ref[...] = s0_ref[0, 0, 0, 0]
        sb_ref[...] = s0_ref[0, 0, 1, 0]

        def fwd_part(r0):
            qc, kc, vc, o = intra(r0)
            qx = (qc.astype(F32) * xi_f).astype(BF16)
            o = o + jnp.dot(qx, sf_ref[...].astype(BF16), preferred_element_type=F32)
            kz = (kc.astype(F32) * zeta_f).astype(BF16)
            sf_ref[...] = gc_f * sf_ref[...] + _dot_tn(kz, vc)
            return o

        def bwd_part(r0):
            qc = q_ref[0, pl.ds(r0, c), ks]
            kc = k_ref[0, pl.ds(r0, c), ks]
            vc = v_ref[0, pl.ds(r0, c), vs]
            qx = (qc.astype(F32) * xi_b).astype(BF16)
            o = jnp.dot(qx, sb_ref[...].astype(BF16), preferred_element_type=F32)
            kz = (kc.astype(F32) * zeta_b).astype(BF16)
            sb_ref[...] = gc_b * sb_ref[...] + _dot_tn(kz, vc)
            return o

        def first_half(t, carry):
            rf = pl.multiple_of(t * c, c)
            rb = pl.multiple_of((nchunks - 1 - t) * c, c)
            oacc_ref[pl.ds(rf, c), :] = fwd_part(rf)
            oacc_ref[pl.ds(rb, c), :] = bwd_part(rb)
            return carry

        def second_half(t, carry):
            rf = pl.multiple_of(t * c, c)
            rb = pl.multiple_of((nchunks - 1 - t) * c, c)
            finalize(rf, oacc_ref[pl.ds(rf, c), :] + fwd_part(rf))
            finalize(rb, oacc_ref[pl.ds(rb, c), :] + bwd_part(rb))
            return carry

        assert nchunks % 2 == 0
        lax.fori_loop(0, nchunks // 2, first_half, 0, unroll=8)
        lax.fori_loop(nchunks // 2, nchunks, second_half, 0, unroll=8)


def _retcore(qkvg, b_off, nb, decay, gnw, state, layer_j, heads, casts=(), states_of=None):
    _, l, _ = qkvg.shape
    has_state = state is not None
    nh = RET_HEADS // heads
    q_off = 0
    k_off = RET_HK // (RET_DK * heads)
    v_off = 2 * RET_HK // (RET_DV * heads)
    g_off = (2 * RET_HK + RET_HV) // (RET_DV * heads)
    in_specs = [
        pl.BlockSpec(memory_space=pltpu.SMEM),
        pl.BlockSpec((1, l, RET_DK * heads), lambda i, h: (b_off + i, 0, q_off + h)),
        pl.BlockSpec((1, l, RET_DK * heads), lambda i, h: (b_off + i, 0, k_off + h)),
        pl.BlockSpec((1, l, RET_DV * heads), lambda i, h: (b_off + i, 0, v_off + h)),
        pl.BlockSpec((1, l, RET_DV * heads), lambda i, h: (b_off + i, 0, g_off + h)),
        pl.BlockSpec((RET_HEADS, 1, RET_DV), lambda i, h: (0, 0, 0)),
    ]
    args = [decay, qkvg, qkvg, qkvg, qkvg, gnw.reshape(RET_HEADS, 1, RET_DV)]
    scratch = []
    if has_state:
        assert heads == 1
        in_specs.append(pl.BlockSpec((1, 1, 2, 1, RET_DK, RET_DV),
                                     lambda i, h: (i, layer_j, 0, h, 0, 0)))
        args.append(state)
        scratch = [pltpu.VMEM((l, RET_DV), F32), pltpu.VMEM((RET_DK, RET_DV), F32),
                   pltpu.VMEM((RET_DK, RET_DV), F32)]
    out_shape = [jax.ShapeDtypeStruct((nb, l, RET_HV), BF16)]
    out_specs = [pl.BlockSpec((1, l, RET_DV * heads), lambda i, h: (i, 0, h))]
    if states_of is not None:
        qkvg_prev, decay_all = states_of
        assert heads == RET_HEADS and not has_state and decay_all.shape[0] == 2
        in_specs += [pl.BlockSpec((1, l, RET_HK), lambda i, h: (b_off + i, 0, 1)),
                     pl.BlockSpec((1, l, RET_HV), lambda i, h: (b_off + i, 0, 1)),
                     pl.BlockSpec(memory_space=pltpu.SMEM)]
        args += [qkvg_prev, qkvg_prev, decay_all]
        out_shape.append(jax.ShapeDtypeStruct((nb, 2, 2, RET_HEADS, RET_DK, RET_DV), F32))
        out_specs.append(pl.BlockSpec((1, 2, 2, RET_HEADS, RET_DK, RET_DV),
                                      lambda i, h: (i, 0, 0, 0, 0, 0)))
    cast_specs = [cs.specs(lambda i, h: i * nh + h) for cs in casts]
    return pl.pallas_call(
        functools.partial(_ret_kernel, heads=heads, has_state=has_state,
                          emit_states=states_of is not None,
                          cast_nblk=tuple(cs.nblk for cs in casts)),
        out_shape=tuple(out_shape) + tuple(cs.out_shape for cs in casts),
        grid=(nb, nh),
        in_specs=in_specs + [s[0] for s in cast_specs],
        out_specs=tuple(out_specs) + tuple(s[1] for s in cast_specs),
        scratch_shapes=scratch,
        compiler_params=_cparams(("arbitrary", "arbitrary")),
        name="retcore_state" if has_state else "retcore",
    )(*args, *[cs.w for cs in casts])


def _outproj_kernel(x_ref, goc_ref, gol_ref, w_ref, g_ref, o_ref, *, n_ctx):
    i = pl.program_id(0)

    def emit(go_ref):
        mix = jnp.dot(go_ref[...], w_ref[...], preferred_element_type=F32)
        o_ref[...] = x_ref[...] + g_ref[0] * mix

    pl.when(i < n_ctx)(lambda: emit(goc_ref))
    pl.when(i >= n_ctx)(lambda: emit(gol_ref))


def _outproj(tk, x2d, go_c, go_l, w_out, ga):
    return pl.pallas_call(
        functools.partial(_outproj_kernel, n_ctx=tk.n_ctx),
        out_shape=jax.ShapeDtypeStruct((tk.t, D_MODEL), F32),
        grid=(tk.nt,),
        in_specs=[
            pl.BlockSpec((tk.tm, D_MODEL), lambda i: (i, 0)),
            pl.BlockSpec((tk.tm, RET_HV), lambda i: (tk.ctx_tile(i), 0)),
            pl.BlockSpec((tk.tm, RET_HV), lambda i: (tk.lat_tile(i), 0)),
            pl.BlockSpec((RET_HV, D_MODEL), lambda i: (0, 0)),
            pl.BlockSpec((1, 1, D_MODEL), lambda i: (tk.mod_row(i), 0, 0)),
        ],
        out_specs=pl.BlockSpec((tk.tm, D_MODEL), lambda i: (i, 0)),
        compiler_params=_cparams(("arbitrary",)),
        name="outproj",
    )(x2d, go_c, go_l, w_out, ga)


def _rope_tables(length):
    quarter = RET_DK // 4
    half = RET_DK // 2
    rows = length // GRID_W
    freqs = ROPE_BASE ** (-jnp.arange(quarter, dtype=F32) / quarter)
    sign = jnp.concatenate([-jnp.ones((quarter,), F32), jnp.ones((quarter,), F32)])

    def tabs(npos):
        ang = jnp.arange(npos, dtype=F32)[:, None] * freqs[None, :]
        return (jnp.concatenate([jnp.cos(ang), jnp.cos(ang)], axis=-1),
                jnp.concatenate([jnp.sin(ang), jnp.sin(ang)], axis=-1) * sign)

    def by_row(tab):
        return jnp.broadcast_to(tab[:, None, :], (rows, GRID_W, half)).reshape(length, half)

    def by_col(tab):
        return jnp.broadcast_to(tab[None, :, :], (rows, GRID_W, half)).reshape(length, half)

    cos_r, sin_r = tabs(rows)
    cos_c, sin_c = tabs(GRID_W)
    return (jnp.concatenate([by_row(cos_r), by_col(cos_c)], axis=-1),
            jnp.concatenate([by_row(sin_r), by_col(sin_c)], axis=-1))


def kernel(x_prompt, x_sample, state_ret, c, c_ctx, w_ada, b_ada, norm_mix_w, norm_mlp_w, pool_w,
           pool_b, pool_scale, ret_w_in, ret_decay, ret_gn_w, ret_w_out, mlp_w1, mlp_w2,
           final_norm_w):
    nb_ctx, seq, _ = x_prompt.shape
    nb_lat, lat, _ = x_sample.shape
    tk = _Tokens(nb_ctx * seq, lat, nb_lat, TM)
    tk_mlp = _Tokens(nb_ctx * seq, lat, nb_lat, TM_MLP)
    assert tk.nc == lat and seq == SLAB

    cond8 = jnp.concatenate([c_ctx[None, :], c, jnp.zeros((8 - 1 - nb_lat, D_MODEL), F32)], axis=0)
    mods = _mods(cond8, w_ada, b_ada)
    mods = mods.reshape(DEPTH, 8, N_MOD, D_MODEL).transpose(0, 2, 1, 3)
    mods = mods[:, :, :1 + nb_lat, None, :]

    cos, sin = _rope_tables(lat)
    inv = jnp.stack([_inv_count_table(lat, False), _inv_count_table(lat, True)])
    pw = pool_w.astype(BF16)
    fw = final_norm_w.reshape(1, D_MODEL)

    def mlp_casts(layer):
        return (_Cast(mlp_w1, layer, 16), _Cast(mlp_w2, layer, 16))

    rsb = _rms(tk, x_prompt.reshape(tk.nc, D_MODEL), x_sample.reshape(nb_lat * lat, D_MODEL))
    x = None
    qkvgs = []
    mlp_w = {}
    for i in range(DEPTH):
        j = i // 2
        sh_a, sc_a, g_a, sh_m, sc_m, g_m = (mods[i, k] for k in range(N_MOD))
        nw_a = norm_mix_w[i].reshape(1, D_MODEL)
        if i % 2 == 0:
            casts = () if i in mlp_w else (_Cast(mlp_w1, i, 8), _Cast(mlp_w2, i, 8))
            if x is None:
                xc3, xl3, xl_off = x_prompt.reshape(1, tk.nc, D_MODEL), x_sample, 1
            else:
                xc3 = xl3 = x.reshape(1 + nb_lat, lat, D_MODEL)
                xl_off = 0
            x, *conv = _pool(xc3, xl3, xl_off, rsb.reshape(1 + nb_lat, lat, LANES),
                             inv, nw_a, sc_a, sh_a, g_a, pool_scale[j].reshape(1, D_MODEL),
                             pool_b[j].reshape(1, D_MODEL), pw, j, casts=casts)
            if conv:
                mlp_w[i] = conv
            x = x.reshape(tk.t, D_MODEL)
            host_casts = (_Cast(ret_w_in, j, 16),)
        else:
            later = [l for l in (i, i + 1) if l < DEPTH]
            qkvg, *conv = _inproj(tk, x, nw_a, sc_a, sh_a, w_in, cos, sin,
                                  casts=sum((mlp_casts(l) for l in later), ()))
            for n, l in enumerate(later):
                mlp_w[l] = conv[2 * n:2 * n + 2]
            qkvgs.append(qkvg)
            states_of = None
            if len(qkvgs) == 2:
                states_of = (qkvgs[0].reshape(tk.t // seq, seq, RET_IN), ret_decay)
            go_c, *new_state = _retcore(qkvg.reshape(tk.t // seq, seq, RET_IN), 0, nb_ctx, ret_decay[j],
                                        ret_gn_w[j], None, j, RET_HEADS, states_of=states_of)
            go_l, w_out = _retcore(qkvg.reshape(tk.t // lat, lat, RET_IN), tk.nc // lat, nb_lat,
                                   ret_decay[j], ret_gn_w[j], state_ret, j, 1,
                                   casts=(_Cast(ret_w_out, j, 8),))
            x = _outproj(tk, x, go_c.reshape(tk.nc, RET_HV), go_l.reshape(nb_lat * lat, RET_HV),
                         w_out, g_a)
            host_casts = ()
        w1, w2 = mlp_w[i]
        x, rsb, *conv = _mlp(tk_mlp, x, norm_mlp_w[i].reshape(1, D_MODEL), sc_m, sh_m, g_m, w1, w2, fw,
                             2048, i == DEPTH - 1, casts=host_casts)
        if conv:
            w_in, = conv
    y_prompt, y_sample = x, rsb
    return (y_prompt.reshape(x_prompt.shape), y_sample.reshape(x_sample.shape), new_state[0])
```

```python
import functools

import jax
import jax.numpy as jnp
from jax import lax
from jax.experimental import pallas as pl
from jax.experimental.pallas import tpu as pltpu

F32 = jnp.float32
BF16 = jnp.bfloat16

D_MODEL = 1024
DEPTH = 4
GRID_W = 64
POOL_WINDOWS = (2, 4, 8, 16)
POOL_GC = 256
RET_HEADS = 4
RET_DK = 256
RET_DV = 512
RET_HK = RET_HEADS * RET_DK
RET_HV = RET_HEADS * RET_DV
RET_IN = 2 * RET_HK + 2 * RET_HV
D_FF = 4 * D_MODEL
ROPE_BASE = 10000.0
NORM_EPS = 1e-6
GN_EPS = 1e-5
N_MOD = 6

LANES = 128
SLAB = 256
TM = 1024
TM_MLP = 1024
VMEM_LIMIT = 56 * 1024 * 1024


def _cparams(sem):
    return pltpu.CompilerParams(dimension_semantics=sem, vmem_limit_bytes=VMEM_LIMIT)


def _norm_mod(x, nw, sc, sh):
    ms = jnp.mean(x * x, axis=-1, keepdims=True)
    return (x * lax.rsqrt(ms + NORM_EPS)) * nw * (1.0 + sc) + sh


def _rsqrt_ms_lanes(x):
    rs = lax.rsqrt(jnp.mean(x * x, axis=-1, keepdims=True) + NORM_EPS)
    return jnp.broadcast_to(rs, (x.shape[0], LANES))


class _Tokens:
    def __init__(self, nc, lat, nlat, tm):
        assert nc % tm == 0 and lat % tm == 0
        self.nc, self.lat, self.nlat, self.tm = nc, lat, nlat, tm
        self.t = nc + lat * nlat
        self.n_ctx = nc // tm
        self.nt = self.t // tm

    def mod_row(self, i):
        return jnp.where(i < self.n_ctx, 0, 1 + (i - self.n_ctx) // (self.lat // self.tm))

    def ctx_tile(self, i):
        return jnp.minimum(i, self.n_ctx - 1)

    def lat_tile(self, i):
        return jnp.maximum(i - self.n_ctx, 0)


class _Cast:
    def __init__(self, w, layer, nblk):
        _, r, c = w.shape
        assert r % (16 * nblk) == 0
        self.w, self.layer, self.nblk, self.rows, self.cols = w, layer, nblk, r // nblk, c
        self.out_shape = jax.ShapeDtypeStruct((r, c), BF16)

    def specs(self, step):
        def blk(*ids):
            return jnp.minimum(step(*ids), self.nblk - 1)
        return (pl.BlockSpec((None, self.rows, self.cols), lambda *ids: (self.layer, blk(*ids), 0)),
                pl.BlockSpec((self.rows, self.cols), lambda *ids: (blk(*ids), 0)))


def _do_casts(src_refs, dst_refs, nblks, step):
    for src_ref, dst_ref, nblk in zip(src_refs, dst_refs, nblks):
        @pl.when(step < nblk)
        def _(src_ref=src_ref, dst_ref=dst_ref):
            dst_ref[...] = src_ref[...].astype(BF16)


def _mods_kernel(c_ref, w_ref, b_ref, o_ref):
    c = c_ref[...]
    s = c * jax.nn.sigmoid(c)
    w = w_ref[0].astype(BF16)
    o_ref[0] = jnp.dot(s.astype(BF16), w, preferred_element_type=F32) + b_ref[0]


def _mods(cond8, w_ada, b_ada):
    tn = 3072
    n = N_MOD * D_MODEL
    return pl.pallas_call(
        _mods_kernel,
        out_shape=jax.ShapeDtypeStruct((DEPTH, 8, n), F32),
        grid=(DEPTH, n // tn),
        in_specs=[
            pl.BlockSpec((8, D_MODEL), lambda l, j: (0, 0)),
            pl.BlockSpec((1, D_MODEL, tn), lambda l, j: (l, 0, j)),
            pl.BlockSpec((1, 1, tn), lambda l, j: (l, 0, j)),
        ],
        out_specs=pl.BlockSpec((1, 8, tn), lambda l, j: (l, 0, j)),
        compiler_params=_cparams(("parallel", "parallel")),
        name="mods",
    )(cond8, w_ada, b_ada.reshape(DEPTH, 1, n))


def _rms_kernel(xc_ref, xl_ref, rs_ref, *, n_ctx):
    i = pl.program_id(0)

    @pl.when(i < n_ctx)
    def _():
        rs_ref[...] = _rsqrt_ms_lanes(xc_ref[...])

    @pl.when(i >= n_ctx)
    def _():
        rs_ref[...] = _rsqrt_ms_lanes(xl_ref[...])


def _rms(tk, xc, xl):
    return pl.pallas_call(
        functools.partial(_rms_kernel, n_ctx=tk.n_ctx),
        out_shape=jax.ShapeDtypeStruct((tk.t, LANES), F32),
        grid=(tk.nt,),
        in_specs=[pl.BlockSpec((tk.tm, D_MODEL), lambda i: (tk.ctx_tile(i), 0)),
                  pl.BlockSpec((tk.tm, D_MODEL), lambda i: (tk.lat_tile(i), 0))],
        out_specs=pl.BlockSpec((tk.tm, LANES), lambda i: (i, 0)),
        compiler_params=_cparams(("arbitrary",)),
        name="rms",
    )(xc, xl)


def _window_count(pos, w, length):
    half = w // 2
    return jnp.minimum(pos - half + w, length) - jnp.maximum(pos - half, 0)


def _inv_count_table(lb, grid):
    t = jnp.arange(lb)
    rows = []
    for w in POOL_WINDOWS:
        if grid:
            cnt = _window_count(t // GRID_W, w, lb // GRID_W) * _window_count(t % GRID_W, w, GRID_W)
        else:
            cnt = _window_count(t % SLAB, w, SLAB)
        rows.append(1.0 / cnt.astype(F32))
    return jnp.broadcast_to(jnp.stack(rows)[:, :, None], (len(POOL_WINDOWS), lb, LANES))


def _pool_group(w, grid, x_ref, rs_ref, inv_ref, nw_ref, sc_ref, sh_ref, ga_ref, ps_ref, pb_ref,
                pw_ref, o_ref, colp_ref, hbuf_ref):
    half = w // 2
    nslab = x_ref.shape[1] // SLAB
    t_i = lax.broadcasted_iota(jnp.int32, (SLAB, SLAB), 0)
    s_i = lax.broadcasted_iota(jnp.int32, (SLAB, SLAB), 1)
    diff = s_i - t_i
    band = (diff >= -half) & (diff <= w - half - 1)
    if grid:
        band = band & ((s_i >> 6) == (t_i >> 6))
    sm = jnp.where(band, 1.0, 0.0).astype(BF16)

    def inv_cnt(r0):
        v = inv_ref[pl.ds(r0, SLAB), :]
        return jnp.concatenate([v, v], axis=1)

    a = nw_ref[...] * (1.0 + sc_ref[0])
    sh = sh_ref[0]
    scale_out = ga_ref[0] * ps_ref[...]
    pw = pw_ref[...]
    pb = pb_ref[...]

    def slab_h(r0):
        x = x_ref[0, pl.ds(r0, SLAB), :]
        rs = rs_ref[0, pl.ds(r0, SLAB), :]
        rs2 = jnp.concatenate([rs, rs], axis=1)
        return x, x * rs2 * a + sh

    def window_sum(h):
        return jnp.dot(sm, h.astype(BF16), preferred_element_type=F32)

    def finish(r0, x, h, m):
        d = (m - h).astype(BF16)
        mix = jnp.dot(d, pw, preferred_element_type=F32) + pb
        o_ref[0, pl.ds(r0, SLAB), :] = x + scale_out * mix

    pad = 8 * GRID_W
    if grid:
        zeros = jnp.zeros((pad, POOL_GC), F32)
        colp_ref[pl.ds(0, pad), :] = zeros
        colp_ref[pl.ds(pad + nslab * SLAB, pad), :] = zeros
    row_offsets = [(j - half) * GRID_W for j in range(w)] if grid else [0]

    def col_body(s, carry):
        r0 = pl.multiple_of(s * SLAB, SLAB)
        _, h = slab_h(r0)
        hbuf_ref[pl.ds(r0, SLAB), :] = h
        colp_ref[pl.ds(pad + r0, SLAB), :] = window_sum(h)
        return carry

    lax.fori_loop(0, nslab, col_body, 0, unroll=16)

    def row_body(s, carry):
        r0 = pl.multiple_of(s * SLAB, SLAB)
        acc = colp_ref[pl.ds(pad + r0 + row_offsets[0], SLAB), :]
        for off in row_offsets[1:]:
            acc = acc + colp_ref[pl.ds(pad + r0 + off, SLAB), :]
        finish(r0, x_ref[0, pl.ds(r0, SLAB), :], hbuf_ref[pl.ds(r0, SLAB), :], acc * inv_cnt(r0))
        return carry

    lax.fori_loop(0, nslab, row_body, 0, unroll=8)


def _pool_kernel(xc_ref, xl_ref, rs_ref, inv_ref, nw_ref, sc_ref, sh_ref, ga_ref, ps_ref, pb_ref, pw_ref,
                 *rest, cast_nblk):
    n_cast = len(cast_nblk)
    cast_in, o_ref, cast_out = rest[:n_cast], rest[n_cast], rest[n_cast + 1:2 * n_cast + 1]
    colp_ref, hbuf_ref = rest[2 * n_cast + 1:]
    b = pl.program_id(0)
    g = pl.program_id(1)
    _do_casts(cast_in, cast_out, cast_nblk, b * pl.num_programs(1) + g)
    for gi, w in enumerate(POOL_WINDOWS):
        for grid in (False, True):
            @pl.when((g == gi) & ((b > 0) if grid else (b == 0)))
            def _(w=w, grid=grid):
                _pool_group(w, grid, xl_ref if grid else xc_ref, rs_ref, inv_ref, nw_ref, sc_ref, sh_ref,
                            ga_ref, ps_ref, pb_ref, pw_ref, o_ref, colp_ref, hbuf_ref)


def _pool(xc3, xl3, xl_off, rsb3, inv, nw, sc, sh, ga, ps, pb, pw_bf16, layer_j, casts=()):
    nb, lb, _ = rsb3.shape
    ng = len(POOL_WINDOWS)
    cast_specs = [cs.specs(lambda b, g: b * ng + g) for cs in casts]
    vec = pl.BlockSpec((1, POOL_GC), lambda b, g: (0, g))
    mod = pl.BlockSpec((1, 1, POOL_GC), lambda b, g: (b, 0, g))
    return pl.pallas_call(
        functools.partial(_pool_kernel, cast_nblk=tuple(cs.nblk for cs in casts)),
        out_shape=(jax.ShapeDtypeStruct((nb, lb, D_MODEL), F32),) + tuple(cs.out_shape for cs in casts),
        grid=(nb, ng),
        in_specs=[
            pl.BlockSpec((1, lb, POOL_GC), lambda b, g: (0, 0, jnp.where(b == 0, g, ng - 1))),
            pl.BlockSpec((1, lb, POOL_GC),
                         lambda b, g: (jnp.maximum(b, 1) - xl_off, 0, jnp.where(b == 0, 0, g))),
            pl.BlockSpec((1, lb, LANES), lambda b, g: (b, 0, 0)),
            pl.BlockSpec((None, None, lb, LANES), lambda b, g: (jnp.minimum(b, 1), g, 0, 0)),
            vec, mod, mod, mod, vec, vec,
            pl.BlockSpec((None, None, POOL_GC, POOL_GC), lambda b, g: (layer_j, g, 0, 0)),
        ] + [s[0] for s in cast_specs],
        out_specs=(pl.BlockSpec((1, lb, POOL_GC), lambda b, g: (b, 0, g)),)
        + tuple(s[1] for s in cast_specs),
        scratch_shapes=[pltpu.VMEM((lb + 16 * GRID_W, POOL_GC), F32), pltpu.VMEM((lb, POOL_GC), F32)],
        compiler_params=_cparams(("arbitrary", "arbitrary")),
        name="pool",
    )(xc3, xl3, rsb3, inv, nw, sc, sh, ga, ps, pb, pw_bf16, *[cs.w for cs in casts])


def _mlp_kernel(x_ref, nw_ref, sc_ref, sh_ref, g_ref, w1_ref, w2_ref, fw_ref, *rest, final, fc, n_ctx,
                cast_nblk):
    n_cast = len(cast_nblk)
    cast_in, rest = rest[:n_cast], rest[n_cast:]
    outs, (h_ref, acc_ref, abuf_ref) = rest[:2 + n_cast], rest[2 + n_cast:]
    if final:
        yc_ref, yl_ref = outs[:2]
    else:
        o_ref, rs_ref = outs[:2]
    i = pl.program_id(0)
    j = pl.program_id(1)
    nj = pl.num_programs(1)

    _do_casts(cast_in, outs[2:], cast_nblk, i * nj + j)

    def ffn(h):
        for c0 in range(0, w1_ref.shape[1], fc):
            a = jnp.dot(h, w1_ref[:, c0:c0 + fc], preferred_element_type=F32)
            a = jnp.maximum(a, 0.0)
            abuf_ref[:, c0:c0 + fc] = (a * a).astype(BF16)
        return jnp.dot(abuf_ref[...], w2_ref[...], preferred_element_type=F32)

    @pl.when(j == 0)
    def _():
        h = _norm_mod(x_ref[...], nw_ref[...], sc_ref[0], sh_ref[0]).astype(BF16)
        h_ref[...] = h
        acc_ref[...] = ffn(h)

    @pl.when((j > 0) & (j < nj - 1))
    def _():
        acc_ref[...] += ffn(h_ref[...])

    @pl.when(j == nj - 1)
    def _():
        y = x_ref[...] + g_ref[0] * (acc_ref[...] + ffn(h_ref[...]))
        if not final:
            o_ref[...] = y
            rs_ref[...] = _rsqrt_ms_lanes(y)
            return
        rs = lax.rsqrt(jnp.mean(y * y, axis=-1, keepdims=True) + NORM_EPS)
        y = (y * rs) * fw_ref[...]

        @pl.when(i < n_ctx)
        def _():
            yc_ref[...] = y

        @pl.when(i >= n_ctx)
        def _():
            yl_ref[...] = y


def _mlp(tk, x2d, nw, sc, sh, g, w1, w2, fw, tf, final, casts=()):
    nj = D_FF // tf
    assert nj >= 2
    cast_specs = [cs.specs(lambda i, j: i * nj + j) for cs in casts]
    vec = pl.BlockSpec((1, D_MODEL), lambda i, j: (0, 0))
    mod = pl.BlockSpec((1, 1, D_MODEL), lambda i, j: (tk.mod_row(i), 0, 0))
    if final:
        out_shape = (jax.ShapeDtypeStruct((tk.nc, D_MODEL), F32),
                     jax.ShapeDtypeStruct((tk.t - tk.nc, D_MODEL), F32))
        out_specs = (pl.BlockSpec((tk.tm, D_MODEL), lambda i, j: (tk.ctx_tile(i), 0)),
                     pl.BlockSpec((tk.tm, D_MODEL), lambda i, j: (tk.lat_tile(i), 0)))
    else:
        out_shape = (jax.ShapeDtypeStruct((tk.t, D_MODEL), F32), jax.ShapeDtypeStruct((tk.t, LANES), F32))
        out_specs = (pl.BlockSpec((tk.tm, D_MODEL), lambda i, j: (i, 0)),
                     pl.BlockSpec((tk.tm, LANES), lambda i, j: (i, 0)))
    return pl.pallas_call(
        functools.partial(_mlp_kernel, final=final, fc=min(tf, 1024), n_ctx=tk.n_ctx,
                          cast_nblk=tuple(cs.nblk for cs in casts)),
        out_shape=out_shape + tuple(cs.out_shape for cs in casts),
        grid=(tk.nt, nj),
        in_specs=[
            pl.BlockSpec((tk.tm, D_MODEL), lambda i, j: (i, 0)),
            vec, mod, mod, mod,
            pl.BlockSpec((D_MODEL, tf), lambda i, j: (0, j)),
            pl.BlockSpec((tf, D_MODEL), lambda i, j: (j, 0)),
            vec,
        ] + [s[0] for s in cast_specs],
        out_specs=out_specs + tuple(s[1] for s in cast_specs),
        scratch_shapes=[pltpu.VMEM((tk.tm, D_MODEL), BF16), pltpu.VMEM((tk.tm, D_MODEL), F32),
                        pltpu.VMEM((tk.tm, tf), BF16)],
        compiler_params=_cparams(("arbitrary", "arbitrary")),
        name="mlp",
    )(x2d, nw, sc, sh, g, w1, w2, fw, *[cs.w for cs in casts])


def _inproj_kernel(x_ref, nw_ref, sc_ref, sh_ref, w_ref, cos_ref, sin_ref, *rest, tc, n_ctx, cast_nblk):
    n_cast = len(cast_nblk)
    cast_in, o_ref, cast_out, h_ref = (rest[:n_cast], rest[n_cast], rest[n_cast + 1:2 * n_cast + 1],
                                       rest[2 * n_cast + 1])
    i = pl.program_id(0)
    j = pl.program_id(1)
    nchunk = o_ref.shape[1] // tc
    _do_casts(cast_in, cast_out, cast_nblk, i * pl.num_programs(1) + j)

    def qk_step(rope):
        h = _norm_mod(x_ref[...], nw_ref[...], sc_ref[0], sh_ref[0]).astype(BF16)
        h_ref[...] = h
        for c in range(nchunk):
            p = jnp.dot(h, w_ref[:, c * tc:(c + 1) * tc], preferred_element_type=F32)
            if c * tc >= RET_HK:
                p = p * (RET_DK ** -0.5)
            if not rope:
                o_ref[:, c * tc:(c + 1) * tc] = p.astype(BF16)
                continue
            for t in range(tc // LANES):
                pt = p[:, t * LANES:(t + 1) * LANES]
                c0 = (t % 2) * LANES
                rot = pltpu.roll(pt, LANES // 2, axis=1)
                pt = pt * cos_ref[:, c0:c0 + LANES] + rot * sin_ref[:, c0:c0 + LANES]
                o_ref[:, c * tc + t * LANES:c * tc + (t + 1) * LANES] = pt.astype(BF16)

    pl.when((j == 0) & (i < n_ctx))(lambda: qk_step(False))
    pl.when((j == 0) & (i >= n_ctx))(lambda: qk_step(True))

    @pl.when(j > 0)
    def _():
        h = h_ref[...]
        scale = jnp.where(j == 2, 0.5, 1.0).astype(F32)
        for c in range(nchunk):
            p = jnp.dot(h, w_ref[:, c * tc:(c + 1) * tc], preferred_element_type=F32)
            o_ref[:, c * tc:(c + 1) * tc] = (p * scale).astype(BF16)


def _inproj(tk, x2d, nw, sc, sh, w_in, cos, sin, casts=()):
    n_tab = cos.shape[0] // tk.tm
    tn = 2 * RET_HK
    nj = RET_IN // tn
    assert RET_HV == tn and RET_IN == 3 * tn
    cast_specs = [cs.specs(lambda i, j: i * nj + j) for cs in casts]
    vec = pl.BlockSpec((1, D_MODEL), lambda i, j: (0, 0))
    mod = pl.BlockSpec((1, 1, D_MODEL), lambda i, j: (tk.mod_row(i), 0, 0))
    tab = pl.BlockSpec((tk.tm, RET_DK), lambda i, j: (tk.lat_tile(i) % n_tab, 0))
    return pl.pallas_call(
        functools.partial(_inproj_kernel, tc=512, n_ctx=tk.n_ctx,
                          cast_nblk=tuple(cs.nblk for cs in casts)),
        out_shape=(jax.ShapeDtypeStruct((tk.t, RET_IN), BF16),) + tuple(cs.out_shape for cs in casts),
        grid=(tk.nt, nj),
        in_specs=[
            pl.BlockSpec((tk.tm, D_MODEL), lambda i, j: (i, 0)),
            vec, mod, mod,
            pl.BlockSpec((D_MODEL, tn), lambda i, j: (0, j)),
            tab, tab,
        ] + [s[0] for s in cast_specs],
        out_specs=(pl.BlockSpec((tk.tm, tn), lambda i, j: (i, j)),) + tuple(s[1] for s in cast_specs),
        scratch_shapes=[pltpu.VMEM((tk.tm, D_MODEL), BF16)],
        compiler_params=_cparams(("arbitrary", "arbitrary")),
        name="inproj",
    )(x2d, nw, sc, sh, w_in, cos, sin, *[cs.w for cs in casts])


def _log_gamma(decay):
    return jnp.log1p(-jnp.exp2(-jnp.full((1, 1), decay, F32)))


def _dot_tn(a, b):
    return lax.dot_general(a, b, (((0,), (0,)), ((), ())), preferred_element_type=F32)


def _dot_nt(a, b):
    return lax.dot_general(a, b, (((1,), (1,)), ((), ())), preferred_element_type=F32)


def _ret_kernel(dec_ref, q_ref, k_ref, v_ref, g_ref, gnw_ref, *rest, heads, has_state, emit_states,
                cast_nblk):
    n_cast = len(cast_nblk)
    if has_state:
        s0_ref, rest = rest[0], rest[1:]
    if emit_states:
        (kp_ref, vp_ref, decall_ref), rest = rest[:3], rest[3:]
    cast_in, o_ref, rest = rest[:n_cast], rest[n_cast], rest[n_cast + 1:]
    if emit_states:
        st_ref, rest = rest[0], rest[1:]
    cast_out, rest = rest[:n_cast], rest[n_cast:]
    if has_state:
        oacc_ref, sf_ref, sb_ref = rest
    _do_casts(cast_in, cast_out, cast_nblk, pl.program_id(0) * pl.num_programs(1) + pl.program_id(1))
    c = SLAB
    nchunks = q_ref.shape[1] // c
    row_k = lax.broadcasted_iota(jnp.int32, (c, RET_DK), 0).astype(F32)
    i_f = lax.broadcasted_iota(jnp.int32, (c, c), 0).astype(F32)
    j_f = lax.broadcasted_iota(jnp.int32, (c, c), 1).astype(F32)
    dif = i_f - j_f

    for hh in range(heads):
        head = pl.program_id(1) * heads + hh
        lgf = _log_gamma(dec_ref[0, head])
        lgb = _log_gamma(dec_ref[1, head])
        dmat = jnp.exp(lgf * jnp.maximum(dif, 0.0) + lgb * jnp.maximum(-dif, 0.0))
        gnw = gnw_ref[head]
        ks = slice(hh * RET_DK, (hh + 1) * RET_DK)
        vs = slice(hh * RET_DV, (hh + 1) * RET_DV)

        def intra(r0):
            qc = q_ref[0, pl.ds(r0, c), ks]
            kc = k_ref[0, pl.ds(r0, c), ks]
            vc = v_ref[0, pl.ds(r0, c), vs]
            p = (_dot_nt(qc, kc) * dmat).astype(BF16)
            return qc, kc, vc, jnp.dot(p, vc, preferred_element_type=F32)

        def finalize(r0, o):
            mu = jnp.mean(o, axis=-1, keepdims=True)
            oc = o - mu
            var = jnp.mean(oc * oc, axis=-1, keepdims=True)
            y = oc * lax.rsqrt(var + GN_EPS) * gnw
            hg = g_ref[0, pl.ds(r0, c), vs].astype(F32)
            o_ref[0, pl.ds(r0, c), vs] = ((hg + hg * jnp.tanh(hg)) * y).astype(BF16)

        if emit_states:
            l = nchunks * c
            row_l = lax.broadcasted_iota(jnp.int32, (l, RET_DK), 0).astype(F32)
            for jj, (kk_ref, vv_ref) in enumerate(((kp_ref, vp_ref), (k_ref, v_ref))):
                lsf = _log_gamma(decall_ref[jj, 0, head])
                lsb = _log_gamma(decall_ref[jj, 1, head])
                kk = kk_ref[0, :, ks].astype(F32)
                vv = vv_ref[0, :, vs]
                st_ref[0, jj, 0, hh] = _dot_tn((kk * jnp.exp(lsf * (l - 1.0 - row_l))).astype(BF16), vv)
                st_ref[0, jj, 1, hh] = _dot_tn((kk * jnp.exp(lsb * row_l)).astype(BF16), vv)

        if not has_state:
            for ci in range(nchunks):
                _, _, _, o = intra(ci * c)
                finalize(ci * c, o)
            continue

        xi_f = jnp.exp(lgf * (row_k + 1.0))
        xi_b = jnp.exp(lgb * (c - row_k))
        zeta_f = jnp.exp(lgf * (c - 1.0 - row_k))
        zeta_b = jnp.exp(lgb * row_k)
        gc_f = jnp.exp(lgf * c)
        gc_b = jnp.exp(lgb * c)
        sf_ref[...] = s0_ref[0, 0, 0, 0]
        sb_ref[...] = s0_ref[0, 0, 1, 0]

        def fwd_part(r0):
            qc, kc, vc, o = intra(r0)
            qx = (qc.astype(F32) * xi_f).astype(BF16)
            o = o + jnp.dot(qx, sf_ref[...].astype(BF16), preferred_element_type=F32)
            kz = (kc.astype(F32) * zeta_f).astype(BF16)
            sf_ref[...] = gc_f * sf_ref[...] + _dot_tn(kz, vc)
            return o

        def bwd_part(r0):
            qc = q_ref[0, pl.ds(r0, c), ks]
            kc = k_ref[0, pl.ds(r0, c), ks]
            vc = v_ref[0, pl.ds(r0, c), vs]
            qx = (qc.astype(F32) * xi_b).astype(BF16)
            o = jnp.dot(qx, sb_ref[...].astype(BF16), preferred_element_type=F32)
            kz = (kc.astype(F32) * zeta_b).astype(BF16)
            sb_ref[...] = gc_b * sb_ref[...] + _dot_tn(kz, vc)
            return o

        def first_half(t, carry):
            rf = pl.multiple_of(t * c, c)
            rb = pl.multiple_of((nchunks - 1 - t) * c, c)
            oacc_ref[pl.ds(rf, c), :] = fwd_part(rf)
            oacc_ref[pl.ds(rb, c), :] = bwd_part(rb)
            return carry

        def second_half(t, carry):
            rf = pl.multiple_of(t * c, c)
            rb = pl.multiple_of((nchunks - 1 - t) * c, c)
            finalize(rf, oacc_ref[pl.ds(rf, c), :] + fwd_part(rf))
            finalize(rb, oacc_ref[pl.ds(rb, c), :] + bwd_part(rb))
            return carry

        assert nchunks % 2 == 0
        lax.fori_loop(0, nchunks // 2, first_half, 0, unroll=8)
        lax.fori_loop(nchunks // 2, nchunks, second_half, 0, unroll=8)


def _retcore(qkvg, b_off, nb, decay, gnw, state, layer_j, heads, casts=(), states_of=None):
    _, l, _ = qkvg.shape
    has_state = state is not None
    nh = RET_HEADS // heads
    q_off = 0
    k_off = RET_HK // (RET_DK * heads)
    v_off = 2 * RET_HK // (RET_DV * heads)
    g_off = (2 * RET_HK + RET_HV) // (RET_DV * heads)
    in_specs = [
        pl.BlockSpec(memory_space=pltpu.SMEM),
        pl.BlockSpec((1, l, RET_DK * heads), lambda i, h: (b_off + i, 0, q_off + h)),
        pl.BlockSpec((1, l, RET_DK * heads), lambda i, h: (b_off + i, 0, k_off + h)),
        pl.BlockSpec((1, l, RET_DV * heads), lambda i, h: (b_off + i, 0, v_off + h)),
        pl.BlockSpec((1, l, RET_DV * heads), lambda i, h: (b_off + i, 0, g_off + h)),
        pl.BlockSpec((RET_HEADS, 1, RET_DV), lambda i, h: (0, 0, 0)),
    ]
    args = [decay, qkvg, qkvg, qkvg, qkvg, gnw.reshape(RET_HEADS, 1, RET_DV)]
    scratch = []
    if has_state:
        assert heads == 1
        in_specs.append(pl.BlockSpec((1, 1, 2, 1, RET_DK, RET_DV),
                                     lambda i, h: (i, layer_j, 0, h, 0, 0)))
        args.append(state)
        scratch = [pltpu.VMEM((l, RET_DV), F32), pltpu.VMEM((RET_DK, RET_DV), F32),
                   pltpu.VMEM((RET_DK, RET_DV), F32)]
    out_shape = [jax.ShapeDtypeStruct((nb, l, RET_HV), BF16)]
    out_specs = [pl.BlockSpec((1, l, RET_DV * heads), lambda i, h: (i, 0, h))]
    if states_of is not None:
        qkvg_prev, decay_all = states_of
        assert heads == RET_HEADS and not has_state and decay_all.shape[0] == 2
        in_specs += [pl.BlockSpec((1, l, RET_HK), lambda i, h: (b_off + i, 0, 1)),
                     pl.BlockSpec((1, l, RET_HV), lambda i, h: (b_off + i, 0, 1)),
                     pl.BlockSpec(memory_space=pltpu.SMEM)]
        args += [qkvg_prev, qkvg_prev, decay_all]
        out_shape.append(jax.ShapeDtypeStruct((nb, 2, 2, RET_HEADS, RET_DK, RET_DV), F32))
        out_specs.append(pl.BlockSpec((1, 2, 2, RET_HEADS, RET_DK, RET_DV),
                                      lambda i, h: (i, 0, 0, 0, 0, 0)))
    cast_specs = [cs.specs(lambda i, h: i * nh + h) for cs in casts]
    return pl.pallas_call(
        functools.partial(_ret_kernel, heads=heads, has_state=has_state,
                          emit_states=states_of is not None,
                          cast_nblk=tuple(cs.nblk for cs in casts)),
        out_shape=tuple(out_shape) + tuple(cs.out_shape for cs in casts),
        grid=(nb, nh),
        in_specs=in_specs + [s[0] for s in cast_specs],
        out_specs=tuple(out_specs) + tuple(s[1] for s in cast_specs),
        scratch_shapes=scratch,
        compiler_params=_cparams(("arbitrary", "arbitrary")),
        name="retcore_state" if has_state else "retcore",
    )(*args, *[cs.w for cs in casts])


def _outproj_kernel(x_ref, goc_ref, gol_ref, w_ref, g_ref, o_ref, *, n_ctx):
    i = pl.program_id(0)

    def emit(go_ref):
        mix = jnp.dot(go_ref[...], w_ref[...], preferred_element_type=F32)
        o_ref[...] = x_ref[...] + g_ref[0] * mix

    pl.when(i < n_ctx)(lambda: emit(goc_ref))
    pl.when(i >= n_ctx)(lambda: emit(gol_ref))


def _outproj(tk, x2d, go_c, go_l, w_out, ga):
    return pl.pallas_call(
        functools.partial(_outproj_kernel, n_ctx=tk.n_ctx),
        out_shape=jax.ShapeDtypeStruct((tk.t, D_MODEL), F32),
        grid=(tk.nt,),
        in_specs=[
            pl.BlockSpec((tk.tm, D_MODEL), lambda i: (i, 0)),
            pl.BlockSpec((tk.tm, RET_HV), lambda i: (tk.ctx_tile(i), 0)),
            pl.BlockSpec((tk.tm, RET_HV), lambda i: (tk.lat_tile(i), 0)),
            pl.BlockSpec((RET_HV, D_MODEL), lambda i: (0, 0)),
            pl.BlockSpec((1, 1, D_MODEL), lambda i: (tk.mod_row(i), 0, 0)),
        ],
        out_specs=pl.BlockSpec((tk.tm, D_MODEL), lambda i: (i, 0)),
        compiler_params=_cparams(("arbitrary",)),
        name="outproj",
    )(x2d, go_c, go_l, w_out, ga)


def _rope_tables(length):
    quarter = RET_DK // 4
    half = RET_DK // 2
    rows = length // GRID_W
    freqs = ROPE_BASE ** (-jnp.arange(quarter, dtype=F32) / quarter)
    sign = jnp.concatenate([-jnp.ones((quarter,), F32), jnp.ones((quarter,), F32)])

    def tabs(npos):
        ang = jnp.arange(npos, dtype=F32)[:, None] * freqs[None, :]
        return (jnp.concatenate([jnp.cos(ang), jnp.cos(ang)], axis=-1),
                jnp.concatenate([jnp.sin(ang), jnp.sin(ang)], axis=-1) * sign)

    def by_row(tab):
        return jnp.broadcast_to(tab[:, None, :], (rows, GRID_W, half)).reshape(length, half)

    def by_col(tab):
        return jnp.broadcast_to(tab[None, :, :], (rows, GRID_W, half)).reshape(length, half)

    cos_r, sin_r = tabs(rows)
    cos_c, sin_c = tabs(GRID_W)
    return (jnp.concatenate([by_row(cos_r), by_col(cos_c)], axis=-1),
            jnp.concatenate([by_row(sin_r), by_col(sin_c)], axis=-1))


def kernel(x_prompt, x_sample, state_ret, c, c_ctx, w_ada, b_ada, norm_mix_w, norm_mlp_w, pool_w,
           pool_b, pool_scale, ret_w_in, ret_decay, ret_gn_w, ret_w_out, mlp_w1, mlp_w2,
           final_norm_w):
    nb_ctx, seq, _ = x_prompt.shape
    nb_lat, lat, _ = x_sample.shape
    tk = _Tokens(nb_ctx * seq, lat, nb_lat, TM)
    tk_mlp = _Tokens(nb_ctx * seq, lat, nb_lat, TM_MLP)
    assert tk.nc == lat and seq == SLAB

    cond8 = jnp.concatenate([c_ctx[None, :], c, jnp.zeros((8 - 1 - nb_lat, D_MODEL), F32)], axis=0)
    mods = _mods(cond8, w_ada, b_ada)
    mods = mods.reshape(DEPTH, 8, N_MOD, D_MODEL).transpose(0, 2, 1, 3)
    mods = mods[:, :, :1 + nb_lat, None, :]

    cos, sin = _rope_tables(lat)
    inv = jnp.stack([_inv_count_table(lat, False), _inv_count_table(lat, True)])
    pw = pool_w.astype(BF16)
    fw = final_norm_w.reshape(1, D_MODEL)

    def mlp_casts(layer):
        return (_Cast(mlp_w1, layer, 16), _Cast(mlp_w2, layer, 16))

    rsb = _rms(tk, x_prompt.reshape(tk.nc, D_MODEL), x_sample.reshape(nb_lat * lat, D_MODEL))
    x = None
    qkvgs = []
    mlp_w = {}
    for i in range(DEPTH):
        j = i // 2
        sh_a, sc_a, g_a, sh_m, sc_m, g_m = (mods[i, k] for k in range(N_MOD))
        nw_a = norm_mix_w[i].reshape(1, D_MODEL)
        if i % 2 == 0:
            casts = () if i in mlp_w else (_Cast(mlp_w1, i, 8), _Cast(mlp_w2, i, 8))
            if x is None:
                xc3, xl3, xl_off = x_prompt.reshape(1, tk.nc, D_MODEL), x_sample, 1
            else:
                xc3 = xl3 = x.reshape(1 + nb_lat, lat, D_MODEL)
                xl_off = 0
            x, *conv = _pool(xc3, xl3, xl_off, rsb.reshape(1 + nb_lat, lat, LANES),
                             inv, nw_a, sc_a, sh_a, g_a, pool_scale[j].reshape(1, D_MODEL),
                             pool_b[j].reshape(1, D_MODEL), pw, j, casts=casts)
            if conv:
                mlp_w[i] = conv
            x = x.reshape(tk.t, D_MODEL)
            host_casts = (_Cast(ret_w_in, j, 16),)
        else:
            later = [l for l in (i, i + 1) if l < DEPTH]
            qkvg, *conv = _inproj(tk, x, nw_a, sc_a, sh_a, w_in, cos, sin,
                                  casts=sum((mlp_casts(l) for l in later), ()))
            for n, l in enumerate(later):
                mlp_w[l] = conv[2 * n:2 * n + 2]
            qkvgs.append(qkvg)
            states_of = None
            if len(qkvgs) == 2:
                states_of = (qkvgs[0].reshape(tk.t // seq, seq, RET_IN), ret_decay)
            go_c, *new_state = _retcore(qkvg.reshape(tk.t // seq, seq, RET_IN), 0, nb_ctx, ret_decay[j],
                                        ret_gn_w[j], None, j, RET_HEADS, states_of=states_of)
            go_l, w_out = _retcore(qkvg.reshape(tk.t // lat, lat, RET_IN), tk.nc // lat, nb_lat,
                                   ret_decay[j], ret_gn_w[j], state_ret, j, 1,
                                   casts=(_Cast(ret_w_out, j, 8),))
            x = _outproj(tk, x, go_c.reshape(tk.nc, RET_HV), go_l.reshape(nb_lat * lat, RET_HV),
                         w_out, g_a)
            host_casts = ()
        w1, w2 = mlp_w[i]
        x, rsb, *conv = _mlp(tk_mlp, x, norm_mlp_w[i].reshape(1, D_MODEL), sc_m, sh_m, g_m, w1, w2, fw,
                             2048, i == DEPTH - 1, casts=host_casts)
        if conv:
            w_in, = conv
    y_prompt, y_sample = x, rsb
    return (y_prompt.reshape(x_prompt.shape), y_sample.reshape(x_sample.shape), new_state[0])
```

```python
import functools

import jax
import jax.numpy as jnp
from jax import lax
from jax.experimental import pallas as pl
from jax.experimental.pallas import tpu as pltpu

F32 = jnp.float32
BF16 = jnp.bfloat16

D_MODEL = 1024
DEPTH = 4
GRID_W = 64
POOL_WINDOWS = (2, 4, 8, 16)
POOL_GC = 256
RET_HEADS = 4
RET_DK = 256
RET_DV = 512
RET_HK = RET_HEADS * RET_DK
RET_HV = RET_HEADS * RET_DV
RET_IN = 2 * RET_HK + 2 * RET_HV
D_FF = 4 * D_MODEL
ROPE_BASE = 10000.0
NORM_EPS = 1e-6
GN_EPS = 1e-5
N_MOD = 6

LANES = 128
SLAB = 256
TM = 1024
TM_MLP = 1024
VMEM_LIMIT = 56 * 1024 * 1024


def _cparams(sem):
    return pltpu.CompilerParams(dimension_semantics=sem, vmem_limit_bytes=VMEM_LIMIT)


def _norm_mod(x, nw, sc, sh):
    ms = jnp.mean(x * x, axis=-1, keepdims=True)
    return (x * lax.rsqrt(ms + NORM_EPS)) * nw * (1.0 + sc) + sh


def _rsqrt_ms_lanes(x):
    rs = lax.rsqrt(jnp.mean(x * x, axis=-1, keepdims=True) + NORM_EPS)
    return jnp.broadcast_to(rs, (x.shape[0], LANES))


class _Tokens:
    def __init__(self, nc, lat, nlat, tm):
        assert nc % tm == 0 and lat % tm == 0
        self.nc, self.lat, self.nlat, self.tm = nc, lat, nlat, tm
        self.t = nc + lat * nlat
        self.n_ctx = nc // tm
        self.nt = self.t // tm

    def mod_row(self, i):
        return jnp.where(i < self.n_ctx, 0, 1 + (i - self.n_ctx) // (self.lat // self.tm))

    def ctx_tile(self, i):
        return jnp.minimum(i, self.n_ctx - 1)

    def lat_tile(self, i):
        return jnp.maximum(i - self.n_ctx, 0)


class _Cast:
    def __init__(self, w, layer, nblk, col_groups=1):
        _, r, c = w.shape
        assert r % (16 * nblk) == 0 and c % (col_groups * LANES) == 0
        self.w, self.layer, self.nblk, self.rows, self.cols = w, layer, nblk, r // nblk, c
        self.groups = col_groups
        shape = (r, c) if col_groups == 1 else (col_groups, r, c // col_groups)
        self.out_shape = jax.ShapeDtypeStruct(shape, BF16)

    def specs(self, step):
        def blk(*ids):
            return jnp.minimum(step(*ids), self.nblk - 1)
        src = pl.BlockSpec((None, self.rows, self.cols), lambda *ids: (self.layer, blk(*ids), 0))
        if self.groups == 1:
            return src, pl.BlockSpec((self.rows, self.cols), lambda *ids: (blk(*ids), 0))
        return src, pl.BlockSpec((self.groups, self.rows, self.cols // self.groups),
                                 lambda *ids: (0, blk(*ids), 0))


def _do_casts(src_refs, dst_refs, nblks, step):
    for src_ref, dst_ref, nblk in zip(src_refs, dst_refs, nblks):
        @pl.when(step < nblk)
        def _(src_ref=src_ref, dst_ref=dst_ref):
            if len(dst_ref.shape) == 2:
                dst_ref[...] = src_ref[...].astype(BF16)
            else:
                cg = dst_ref.shape[2]
                for g in range(dst_ref.shape[0]):
                    dst_ref[g] = src_ref[:, g * cg:(g + 1) * cg].astype(BF16)


def _mods_kernel(c_ref, w_ref, b_ref, o_ref):
    c = c_ref[...]
    s = c * jax.nn.sigmoid(c)
    w = w_ref[0].astype(BF16)
    o_ref[0] = jnp.dot(s.astype(BF16), w, preferred_element_type=F32) + b_ref[0]


def _mods(cond8, w_ada, b_ada):
    tn = 3072
    n = N_MOD * D_MODEL
    return pl.pallas_call(
        _mods_kernel,
        out_shape=jax.ShapeDtypeStruct((DEPTH, 8, n), F32),
        grid=(DEPTH, n // tn),
        in_specs=[
            pl.BlockSpec((8, D_MODEL), lambda l, j: (0, 0)),
            pl.BlockSpec((1, D_MODEL, tn), lambda l, j: (l, 0, j)),
            pl.BlockSpec((1, 1, tn), lambda l, j: (l, 0, j)),
        ],
        out_specs=pl.BlockSpec((1, 8, tn), lambda l, j: (l, 0, j)),
        compiler_params=_cparams(("parallel", "parallel")),
        name="mods",
    )(cond8, w_ada, b_ada.reshape(DEPTH, 1, n))


def _rms_kernel(xc_ref, xl_ref, rs_ref, *, n_ctx):
    i = pl.program_id(0)

    @pl.when(i < n_ctx)
    def _():
        rs_ref[...] = _rsqrt_ms_lanes(xc_ref[...])

    @pl.when(i >= n_ctx)
    def _():
        rs_ref[...] = _rsqrt_ms_lanes(xl_ref[...])


def _rms(tk, xc, xl):
    return pl.pallas_call(
        functools.partial(_rms_kernel, n_ctx=tk.n_ctx),
        out_shape=jax.ShapeDtypeStruct((tk.t, LANES), F32),
        grid=(tk.nt,),
        in_specs=[pl.BlockSpec((tk.tm, D_MODEL), lambda i: (tk.ctx_tile(i), 0)),
                  pl.BlockSpec((tk.tm, D_MODEL), lambda i: (tk.lat_tile(i), 0))],
        out_specs=pl.BlockSpec((tk.tm, LANES), lambda i: (i, 0)),
        compiler_params=_cparams(("arbitrary",)),
        name="rms",
    )(xc, xl)


def _window_count(pos, w, length):
    half = w // 2
    return jnp.minimum(pos - half + w, length) - jnp.maximum(pos - half, 0)


def _inv_count_table(lb, grid):
    t = jnp.arange(lb)
    rows = []
    for w in POOL_WINDOWS:
        if grid:
            cnt = _window_count(t // GRID_W, w, lb // GRID_W) * _window_count(t % GRID_W, w, GRID_W)
        else:
            cnt = _window_count(t % SLAB, w, SLAB)
        rows.append(1.0 / cnt.astype(F32))
    return jnp.broadcast_to(jnp.stack(rows)[:, :, None], (len(POOL_WINDOWS), lb, LANES))


def _pool_group(w, grid, x_ref, rs_ref, inv_ref, nw_ref, sc_ref, sh_ref, ga_ref, ps_ref, pb_ref,
                pw_ref, o_ref, colp_ref, hbuf_ref):
    half = w // 2
    nslab = x_ref.shape[1] // SLAB
    t_i = lax.broadcasted_iota(jnp.int32, (SLAB, SLAB), 0)
    s_i = lax.broadcasted_iota(jnp.int32, (SLAB, SLAB), 1)
    diff = s_i - t_i
    band = (diff >= -half) & (diff <= w - half - 1)
    if grid:
        band = band & ((s_i >> 6) == (t_i >> 6))
    sm = jnp.where(band, 1.0, 0.0).astype(BF16)

    def inv_cnt(r0):
        v = inv_ref[pl.ds(r0, SLAB), :]
        return jnp.concatenate([v, v], axis=1)

    a = nw_ref[...] * (1.0 + sc_ref[0])
    sh = sh_ref[0]
    scale_out = ga_ref[0] * ps_ref[...]
    pw = pw_ref[...]
    pb = pb_ref[...]

    def slab_h(r0):
        x = x_ref[0, pl.ds(r0, SLAB), :]
        rs = rs_ref[0, pl.ds(r0, SLAB), :]
        rs2 = jnp.concatenate([rs, rs], axis=1)
        return x, x * rs2 * a + sh

    def window_sum(h):
        return jnp.dot(sm, h.astype(BF16), preferred_element_type=F32)

    def finish(r0, x, h, m):
        d = (m - h).astype(BF16)
        mix = jnp.dot(d, pw, preferred_element_type=F32) + pb
        o_ref[0, pl.ds(r0, SLAB), :] = x + scale_out * mix

    pad = 8 * GRID_W
    if grid:
        zeros = jnp.zeros((pad, POOL_GC), F32)
        colp_ref[pl.ds(0, pad), :] = zeros
        colp_ref[pl.ds(pad + nslab * SLAB, pad), :] = zeros
    row_offsets = [(j - half) * GRID_W for j in range(w)] if grid else [0]

    def col_body(s, carry):
        r0 = pl.multiple_of(s * SLAB, SLAB)
        _, h = slab_h(r0)
        hbuf_ref[pl.ds(r0, SLAB), :] = h
        colp_ref[pl.ds(pad + r0, SLAB), :] = window_sum(h)
        return carry

    lax.fori_loop(0, nslab, col_body, 0, unroll=16)

    def row_body(s, carry):
        r0 = pl.multiple_of(s * SLAB, SLAB)
        acc = colp_ref[pl.ds(pad + r0 + row_offsets[0], SLAB), :]
        for off in row_offsets[1:]:
            acc = acc + colp_ref[pl.ds(pad + r0 + off, SLAB), :]
        finish(r0, x_ref[0, pl.ds(r0, SLAB), :], hbuf_ref[pl.ds(r0, SLAB), :], acc * inv_cnt(r0))
        return carry

    lax.fori_loop(0, nslab, row_body, 0, unroll=8)


def _pool_kernel(xc_ref, xl_ref, rs_ref, inv_ref, nw_ref, sc_ref, sh_ref, ga_ref, ps_ref, pb_ref, pw_ref,
                 *rest, cast_nblk):
    n_cast = len(cast_nblk)
    cast_in, o_ref, cast_out = rest[:n_cast], rest[n_cast], rest[n_cast + 1:2 * n_cast + 1]
    colp_ref, hbuf_ref = rest[2 * n_cast + 1:]
    b = pl.program_id(0)
    g = pl.program_id(1)
    _do_casts(cast_in, cast_out, cast_nblk, b * pl.num_programs(1) + g)
    for gi, w in enumerate(POOL_WINDOWS):
        for grid in (False, True):
            @pl.when((g == gi) & ((b > 0) if grid else (b == 0)))
            def _(w=w, grid=grid):
                _pool_group(w, grid, xl_ref if grid else xc_ref, rs_ref, inv_ref, nw_ref, sc_ref, sh_ref,
                            ga_ref, ps_ref, pb_ref, pw_ref, o_ref, colp_ref, hbuf_ref)


def _pool(xc3, xl3, xl_off, rsb3, inv, nw, sc, sh, ga, ps, pb, pw_bf16, layer_j, casts=()):
    nb, lb, _ = rsb3.shape
    ng = len(POOL_WINDOWS)
    cast_specs = [cs.specs(lambda b, g: b * ng + g) for cs in casts]
    vec = pl.BlockSpec((1, POOL_GC), lambda b, g: (0, g))
    mod = pl.BlockSpec((1, 1, POOL_GC), lambda b, g: (b, 0, g))
    return pl.pallas_call(
        functools.partial(_pool_kernel, cast_nblk=tuple(cs.nblk for cs in casts)),
        out_shape=(jax.ShapeDtypeStruct((nb, lb, D_MODEL), F32),) + tuple(cs.out_shape for cs in casts),
        grid=(nb, ng),
        in_specs=[
            pl.BlockSpec((1, lb, POOL_GC), lambda b, g: (0, 0, jnp.where(b == 0, g, ng - 1))),
            pl.BlockSpec((1, lb, POOL_GC),
                         lambda b, g: (jnp.maximum(b, 1) - xl_off, 0, jnp.where(b == 0, 0, g))),
            pl.BlockSpec((1, lb, LANES), lambda b, g: (b, 0, 0)),
            pl.BlockSpec((None, None, lb, LANES), lambda b, g: (jnp.minimum(b, 1), g, 0, 0)),
            vec, mod, mod, mod, vec, vec,
            pl.BlockSpec((None, None, POOL_GC, POOL_GC), lambda b, g: (layer_j, g, 0, 0)),
        ] + [s[0] for s in cast_specs],
        out_specs=(pl.BlockSpec((1, lb, POOL_GC), lambda b, g: (b, 0, g)),)
        + tuple(s[1] for s in cast_specs),
        scratch_shapes=[pltpu.VMEM((lb + 16 * GRID_W, POOL_GC), F32), pltpu.VMEM((lb, POOL_GC), F32)],
        compiler_params=_cparams(("arbitrary", "arbitrary")),
        name="pool",
    )(xc3, xl3, rsb3, inv, nw, sc, sh, ga, ps, pb, pw_bf16, *[cs.w for cs in casts])


def _mlp_kernel(x_ref, nw_ref, sc_ref, sh_ref, g_ref, w1_ref, w2_ref, fw_ref, *rest, final, fc, n_ctx,
                cast_nblk):
    n_cast = len(cast_nblk)
    cast_in, rest = rest[:n_cast], rest[n_cast:]
    outs, (h_ref, acc_ref, abuf_ref) = rest[:2 + n_cast], rest[2 + n_cast:]
    if final:
        yc_ref, yl_ref = outs[:2]
    else:
        o_ref, rs_ref = outs[:2]
    i = pl.program_id(0)
    j = pl.program_id(1)
    nj = pl.num_programs(1)

    _do_casts(cast_in, outs[2:], cast_nblk, i * nj + j)

    def ffn(h):
        for c0 in range(0, w1_ref.shape[1], fc):
            a = jnp.dot(h, w1_ref[:, c0:c0 + fc], preferred_element_type=F32)
            a = jnp.maximum(a, 0.0)
            abuf_ref[:, c0:c0 + fc] = (a * a).astype(BF16)
        return jnp.dot(abuf_ref[...], w2_ref[...], preferred_element_type=F32)

    @pl.when(j == 0)
    def _():
        h = _norm_mod(x_ref[...], nw_ref[...], sc_ref[0], sh_ref[0]).astype(BF16)
        h_ref[...] = h
        acc_ref[...] = ffn(h)

    @pl.when((j > 0) & (j < nj - 1))
    def _():
        acc_ref[...] += ffn(h_ref[...])

    @pl.when(j == nj - 1)
    def _():
        y = x_ref[...] + g_ref[0] * (acc_ref[...] + ffn(h_ref[...]))
        if not final:
            o_ref[...] = y
            rs_ref[...] = _rsqrt_ms_lanes(y)
            return
        rs = lax.rsqrt(jnp.mean(y * y, axis=-1, keepdims=True) + NORM_EPS)
        y = (y * rs) * fw_ref[...]

        @pl.when(i < n_ctx)
        def _():
            yc_ref[...] = y

        @pl.when(i >= n_ctx)
        def _():
            yl_ref[...] = y


def _mlp(tk, x2d, nw, sc, sh, g, w1, w2, fw, tf, final, casts=()):
    nj = D_FF // tf
    assert nj >= 2
    cast_specs = [cs.specs(lambda i, j: i * nj + j) for cs in casts]
    vec = pl.BlockSpec((1, D_MODEL), lambda i, j: (0, 0))
    mod = pl.BlockSpec((1, 1, D_MODEL), lambda i, j: (tk.mod_row(i), 0, 0))
    if final:
        out_shape = (jax.ShapeDtypeStruct((tk.nc, D_MODEL), F32),
                     jax.ShapeDtypeStruct((tk.t - tk.nc, D_MODEL), F32))
        out_specs = (pl.BlockSpec((tk.tm, D_MODEL), lambda i, j: (tk.ctx_tile(i), 0)),
                     pl.BlockSpec((tk.tm, D_MODEL), lambda i, j: (tk.lat_tile(i), 0)))
    else:
        out_shape = (jax.ShapeDtypeStruct((tk.t, D_MODEL), F32), jax.ShapeDtypeStruct((tk.t, LANES), F32))
        out_specs = (pl.BlockSpec((tk.tm, D_MODEL), lambda i, j: (i, 0)),
                     pl.BlockSpec((tk.tm, LANES), lambda i, j: (i, 0)))
    return pl.pallas_call(
        functools.partial(_mlp_kernel, final=final, fc=min(tf, 1024), n_ctx=tk.n_ctx,
                          cast_nblk=tuple(cs.nblk for cs in casts)),
        out_shape=out_shape + tuple(cs.out_shape for cs in casts),
        grid=(tk.nt, nj),
        in_specs=[
            pl.BlockSpec((tk.tm, D_MODEL), lambda i, j: (i, 0)),
            vec, mod, mod, mod,
            pl.BlockSpec((D_MODEL, tf), lambda i, j: (0, j)),
            pl.BlockSpec((tf, D_MODEL), lambda i, j: (j, 0)),
            vec,
        ] + [s[0] for s in cast_specs],
        out_specs=out_specs + tuple(s[1] for s in cast_specs),
        scratch_shapes=[pltpu.VMEM((tk.tm, D_MODEL), BF16), pltpu.VMEM((tk.tm, D_MODEL), F32),
                        pltpu.VMEM((tk.tm, tf), BF16)],
        compiler_params=_cparams(("arbitrary", "arbitrary")),
        name="mlp",
    )(x2d, nw, sc, sh, g, w1, w2, fw, *[cs.w for cs in casts])


def _inproj_kernel(x_ref, nw_ref, sc_ref, sh_ref, w_ref, cos_ref, sin_ref, *rest, tc, n_ctx, cast_nblk):
    n_cast = len(cast_nblk)
    cast_in, o_ref, cast_out, h_ref = (rest[:n_cast], rest[n_cast], rest[n_cast + 1:2 * n_cast + 1],
                                       rest[2 * n_cast + 1])
    i = pl.program_id(0)
    j = pl.program_id(1)
    nchunk = o_ref.shape[1] // tc
    _do_casts(cast_in, cast_out, cast_nblk, i * pl.num_programs(1) + j)

    def qk_step(rope):
        h = _norm_mod(x_ref[...], nw_ref[...], sc_ref[0], sh_ref[0]).astype(BF16)
        h_ref[...] = h
        for c in range(nchunk):
            p = jnp.dot(h, w_ref[0, :, c * tc:(c + 1) * tc], preferred_element_type=F32)
            if c * tc >= RET_HK:
                p = p * (RET_DK ** -0.5)
            if not rope:
                o_ref[:, c * tc:(c + 1) * tc] = p.astype(BF16)
                continue
            for t in range(tc // LANES):
                pt = p[:, t * LANES:(t + 1) * LANES]
                c0 = (t % 2) * LANES
                rot = pltpu.roll(pt, LANES // 2, axis=1)
                pt = pt * cos_ref[:, c0:c0 + LANES] + rot * sin_ref[:, c0:c0 + LANES]
                o_ref[:, c * tc + t * LANES:c * tc + (t + 1) * LANES] = pt.astype(BF16)

    pl.when((j == 0) & (i < n_ctx))(lambda: qk_step(False))
    pl.when((j == 0) & (i >= n_ctx))(lambda: qk_step(True))

    @pl.when(j > 0)
    def _():
        h = h_ref[...]
        scale = jnp.where(j == 2, 0.5, 1.0).astype(F32)
        for c in range(nchunk):
            p = jnp.dot(h, w_ref[j, :, c * tc:(c + 1) * tc], preferred_element_type=F32)
            o_ref[:, c * tc:(c + 1) * tc] = (p * scale).astype(BF16)


def _inproj(tk, x2d, nw, sc, sh, w_in, cos, sin, casts=()):
    n_tab = cos.shape[0] // tk.tm
    tn = 2 * RET_HK
    nj = RET_IN // tn
    assert RET_HV == tn and RET_IN == 3 * tn
    cast_specs = [cs.specs(lambda i, j: i * nj + j) for cs in casts]
    vec = pl.BlockSpec((1, D_MODEL), lambda i, j: (0, 0))
    mod = pl.BlockSpec((1, 1, D_MODEL), lambda i, j: (tk.mod_row(i), 0, 0))
    tab = pl.BlockSpec((tk.tm, RET_DK), lambda i, j: (tk.lat_tile(i) % n_tab, 0))
    return pl.pallas_call(
        functools.partial(_inproj_kernel, tc=512, n_ctx=tk.n_ctx,
                          cast_nblk=tuple(cs.nblk for cs in casts)),
        out_shape=(jax.ShapeDtypeStruct((tk.t, RET_IN), BF16),) + tuple(cs.out_shape for cs in casts),
        grid=(tk.nt, nj),
        in_specs=[
            pl.BlockSpec((tk.tm, D_MODEL), lambda i, j: (i, 0)),
            vec, mod, mod,
            pl.BlockSpec((nj, D_MODEL, tn), lambda i, j: (0, 0, 0), pipeline_mode=pl.Buffered(1)),
            tab, tab,
        ] + [s[0] for s in cast_specs],
        out_specs=(pl.BlockSpec((tk.tm, tn), lambda i, j: (i, j)),) + tuple(s[1] for s in cast_specs),
        scratch_shapes=[pltpu.VMEM((tk.tm, D_MODEL), BF16)],
        compiler_params=_cparams(("arbitrary", "arbitrary")),
        name="inproj",
    )(x2d, nw, sc, sh, w_in, cos, sin, *[cs.w for cs in casts])


def _log_gamma(decay):
    return jnp.log1p(-jnp.exp2(-jnp.full((1, 1), decay, F32)))


def _dot_tn(a, b):
    return lax.dot_general(a, b, (((0,), (0,)), ((), ())), preferred_element_type=F32)


def _dot_nt(a, b):
    return lax.dot_general(a, b, (((1,), (1,)), ((), ())), preferred_element_type=F32)


def _ret_kernel(dec_ref, q_ref, k_ref, v_ref, g_ref, gnw_ref, *rest, heads, has_state, emit_states,
                cast_nblk):
    n_cast = len(cast_nblk)
    if has_state:
        s0_ref, rest = rest[0], rest[1:]
    if emit_states:
        (kp_ref, vp_ref, decall_ref), rest = rest[:3], rest[3:]
    cast_in, o_ref, rest = rest[:n_cast], rest[n_cast], rest[n_cast + 1:]
    if emit_states:
        st_ref, rest = rest[0], rest[1:]
    cast_out, rest = rest[:n_cast], rest[n_cast:]
    if has_state:
        oacc_ref, sf_ref, sb_ref = rest
    _do_casts(cast_in, cast_out, cast_nblk, pl.program_id(0) * pl.num_programs(1) + pl.program_id(1))
    c = SLAB
    nchunks = q_ref.shape[1] // c
    row_k = lax.broadcasted_iota(jnp.int32, (c, RET_DK), 0).astype(F32)
    i_f = lax.broadcasted_iota(jnp.int32, (c, c), 0).astype(F32)
    j_f = lax.broadcasted_iota(jnp.int32, (c, c), 1).astype(F32)
    dif = i_f - j_f

    for hh in range(heads):
        head = pl.program_id(1) * heads + hh
        lgf = _log_gamma(dec_ref[0, head])
        lgb = _log_gamma(dec_ref[1, head])
        dmat = jnp.exp(lgf * jnp.maximum(dif, 0.0) + lgb * jnp.maximum(-dif, 0.0))
        gnw = gnw_ref[head]
        ks = slice(hh * RET_DK, (hh + 1) * RET_DK)
        vs = slice(hh * RET_DV, (hh + 1) * RET_DV)

        def intra(r0):
            qc = q_ref[0, pl.ds(r0, c), ks]
            kc = k_ref[0, pl.ds(r0, c), ks]
            vc = v_ref[0, pl.ds(r0, c), vs]
            p = (_dot_nt(qc, kc) * dmat).astype(BF16)
            return qc, kc, vc, jnp.dot(p, vc, preferred_element_type=F32)

        def finalize(r0, o):
            mu = jnp.mean(o, axis=-1, keepdims=True)
            oc = o - mu
            var = jnp.mean(oc * oc, axis=-1, keepdims=True)
            y = oc * lax.rsqrt(var + GN_EPS) * gnw
            hg = g_ref[0, pl.ds(r0, c), vs].astype(F32)
            o_ref[0, pl.ds(r0, c), vs] = ((hg + hg * jnp.tanh(hg)) * y).astype(BF16)

        if emit_states:
            l = nchunks * c
            row_l = lax.broadcasted_iota(jnp.int32, (l, RET_DK), 0).astype(F32)
            for jj, (kk_ref, vv_ref) in enumerate(((kp_ref, vp_ref), (k_ref, v_ref))):
                lsf = _log_gamma(decall_ref[jj, 0, head])
                lsb = _log_gamma(decall_ref[jj, 1, head])
                kk = kk_ref[0, :, ks].astype(F32)
                vv = vv_ref[0, :, vs]
                st_ref[0, jj, 0, hh] = _dot_tn((kk * jnp.exp(lsf * (l - 1.0 - row_l))).astype(BF16), vv)
                st_ref[0, jj, 1, hh] = _dot_tn((kk * jnp.exp(lsb * row_l)).astype(BF16), vv)

        if not has_state:
            for ci in range(nchunks):
                _, _, _, o = intra(ci * c)
                finalize(ci * c, o)
            continue

        xi_f = jnp.exp(lgf * (row_k + 1.0))
        xi_b = jnp.exp(lgb * (c - row_k))
        zeta_f = jnp.exp(lgf * (c - 1.0 - row_k))
        zeta_b = jnp.exp(lgb * row_k)
        gc_f = jnp.exp(lgf * c)
        gc_b = jnp.exp(lgb * c)
        sf_ref[...] = s0_ref[0, 0, 0, 0]
        sb_ref[...] = s0_ref[0, 0, 1, 0]

        def fwd_part(r0):
            qc, kc, vc, o = intra(r0)
            qx = (qc.astype(F32) * xi_f).astype(BF16)
            o = o + jnp.dot(qx, sf_ref[...].astype(BF16), preferred_element_type=F32)
            kz = (kc.astype(F32) * zeta_f).astype(BF16)
            sf_ref[...] = gc_f * sf_ref[...] + _dot_tn(kz, vc)
            return o

        def bwd_part(r0):
            qc = q_ref[0, pl.ds(r0, c), ks]
            kc = k_ref[0, pl.ds(r0, c), ks]
            vc = v_ref[0, pl.ds(r0, c), vs]
            qx = (qc.astype(F32) * xi_b).astype(BF16)
            o = jnp.dot(qx, sb_ref[...].astype(BF16), preferred_element_type=F32)
            kz = (kc.astype(F32) * zeta_b).astype(BF16)
            sb_ref[...] = gc_b * sb_ref[...] + _dot_tn(kz, vc)
            return o

        def first_half(t, carry):
            rf = pl.multiple_of(t * c, c)
            rb = pl.multiple_of((nchunks - 1 - t) * c, c)
            oacc_ref[pl.ds(rf, c), :] = fwd_part(rf)
            oacc_ref[pl.ds(rb, c), :] = bwd_part(rb)
            return carry

        def second_half(t, carry):
            rf = pl.multiple_of(t * c, c)
            rb = pl.multiple_of((nchunks - 1 - t) * c, c)
            finalize(rf, oacc_ref[pl.ds(rf, c), :] + fwd_part(rf))
            finalize(rb, oacc_ref[pl.ds(rb, c), :] + bwd_part(rb))
            return carry

        assert nchunks % 2 == 0
        lax.fori_loop(0, nchunks // 2, first_half, 0, unroll=8)
        lax.fori_loop(nchunks // 2, nchunks, second_half, 0, unroll=8)


def _retcore(qkvg, b_off, nb, decay, gnw, state, layer_j, heads, casts=(), states_of=None):
    _, l, _ = qkvg.shape
    has_state = state is not None
    nh = RET_HEADS // heads
    q_off = 0
    k_off = RET_HK // (RET_DK * heads)
    v_off = 2 * RET_HK // (RET_DV * heads)
    g_off = (2 * RET_HK + RET_HV) // (RET_DV * heads)
    in_specs = [
        pl.BlockSpec(memory_space=pltpu.SMEM),
        pl.BlockSpec((1, l, RET_DK * heads), lambda i, h: (b_off + i, 0, q_off + h)),
        pl.BlockSpec((1, l, RET_DK * heads), lambda i, h: (b_off + i, 0, k_off + h)),
        pl.BlockSpec((1, l, RET_DV * heads), lambda i, h: (b_off + i, 0, v_off + h)),
        pl.BlockSpec((1, l, RET_DV * heads), lambda i, h: (b_off + i, 0, g_off + h)),
        pl.BlockSpec((RET_HEADS, 1, RET_DV), lambda i, h: (0, 0, 0)),
    ]
    args = [decay, qkvg, qkvg, qkvg, qkvg, gnw.reshape(RET_HEADS, 1, RET_DV)]
    scratch = []
    if has_state:
        assert heads == 1
        in_specs.append(pl.BlockSpec((1, 1, 2, 1, RET_DK, RET_DV),
                                     lambda i, h: (i, layer_j, 0, h, 0, 0)))
        args.append(state)
        scratch = [pltpu.VMEM((l, RET_DV), F32), pltpu.VMEM((RET_DK, RET_DV), F32),
                   pltpu.VMEM((RET_DK, RET_DV), F32)]
    out_shape = [jax.ShapeDtypeStruct((nb, l, RET_HV), BF16)]
    out_specs = [pl.BlockSpec((1, l, RET_DV * heads), lambda i, h: (i, 0, h))]
    if states_of is not None:
        qkvg_prev, decay_all = states_of
        assert heads == RET_HEADS and not has_state and decay_all.shape[0] == 2
        in_specs += [pl.BlockSpec((1, l, RET_HK), lambda i, h: (b_off + i, 0, 1)),
                     pl.BlockSpec((1, l, RET_HV), lambda i, h: (b_off + i, 0, 1)),
                     pl.BlockSpec(memory_space=pltpu.SMEM)]
        args += [qkvg_prev, qkvg_prev, decay_all]
        out_shape.append(jax.ShapeDtypeStruct((nb, 2, 2, RET_HEADS, RET_DK, RET_DV), F32))
        out_specs.append(pl.BlockSpec((1, 2, 2, RET_HEADS, RET_DK, RET_DV),
                                      lambda i, h: (i, 0, 0, 0, 0, 0)))
    cast_specs = [cs.specs(lambda i, h: i * nh + h) for cs in casts]
    return pl.pallas_call(
        functools.partial(_ret_kernel, heads=heads, has_state=has_state,
                          emit_states=states_of is not None,
                          cast_nblk=tuple(cs.nblk for cs in casts)),
        out_shape=tuple(out_shape) + tuple(cs.out_shape for cs in casts),
        grid=(nb, nh),
        in_specs=in_specs + [s[0] for s in cast_specs],
        out_specs=tuple(out_specs) + tuple(s[1] for s in cast_specs),
        scratch_shapes=scratch,
        compiler_params=_cparams(("arbitrary", "arbitrary")),
        name="retcore_state" if has_state else "retcore",
    )(*args, *[cs.w for cs in casts])


def _outproj_kernel(x_ref, goc_ref, gol_ref, w_ref, g_ref, o_ref, *, n_ctx):
    i = pl.program_id(0)

    def emit(go_ref):
        mix = jnp.dot(go_ref[...], w_ref[...], preferred_element_type=F32)
        o_ref[...] = x_ref[...] + g_ref[0] * mix

    pl.when(i < n_ctx)(lambda: emit(goc_ref))
    pl.when(i >= n_ctx)(lambda: emit(gol_ref))


def _outproj(tk, x2d, go_c, go_l, w_out, ga):
    return pl.pallas_call(
        functools.partial(_outproj_kernel, n_ctx=tk.n_ctx),
        out_shape=jax.ShapeDtypeStruct((tk.t, D_MODEL), F32),
        grid=(tk.nt,),
        in_specs=[
            pl.BlockSpec((tk.tm, D_MODEL), lambda i: (i, 0)),
            pl.BlockSpec((tk.tm, RET_HV), lambda i: (tk.ctx_tile(i), 0)),
            pl.BlockSpec((tk.tm, RET_HV), lambda i: (tk.lat_tile(i), 0)),
            pl.BlockSpec((RET_HV, D_MODEL), lambda i: (0, 0)),
            pl.BlockSpec((1, 1, D_MODEL), lambda i: (tk.mod_row(i), 0, 0)),
        ],
        out_specs=pl.BlockSpec((tk.tm, D_MODEL), lambda i: (i, 0)),
        compiler_params=_cparams(("arbitrary",)),
        name="outproj",
    )(x2d, go_c, go_l, w_out, ga)


def _rope_tables(length):
    quarter = RET_DK // 4
    half = RET_DK // 2
    rows = length // GRID_W
    freqs = ROPE_BASE ** (-jnp.arange(quarter, dtype=F32) / quarter)
    sign = jnp.concatenate([-jnp.ones((quarter,), F32), jnp.ones((quarter,), F32)])

    def tabs(npos):
        ang = jnp.arange(npos, dtype=F32)[:, None] * freqs[None, :]
        return (jnp.concatenate([jnp.cos(ang), jnp.cos(ang)], axis=-1),
                jnp.concatenate([jnp.sin(ang), jnp.sin(ang)], axis=-1) * sign)

    def by_row(tab):
        return jnp.broadcast_to(tab[:, None, :], (rows, GRID_W, half)).reshape(length, half)

    def by_col(tab):
        return jnp.broadcast_to(tab[None, :, :], (rows, GRID_W, half)).reshape(length, half)

    cos_r, sin_r = tabs(rows)
    cos_c, sin_c = tabs(GRID_W)
    return (jnp.concatenate([by_row(cos_r), by_col(cos_c)], axis=-1),
            jnp.concatenate([by_row(sin_r), by_col(sin_c)], axis=-1))


def kernel(x_prompt, x_sample, state_ret, c, c_ctx, w_ada, b_ada, norm_mix_w, norm_mlp_w, pool_w,
           pool_b, pool_scale, ret_w_in, ret_decay, ret_gn_w, ret_w_out, mlp_w1, mlp_w2,
           final_norm_w):
    nb_ctx, seq, _ = x_prompt.shape
    nb_lat, lat, _ = x_sample.shape
    tk = _Tokens(nb_ctx * seq, lat, nb_lat, TM)
    tk_mlp = _Tokens(nb_ctx * seq, lat, nb_lat, TM_MLP)
    assert tk.nc == lat and seq == SLAB

    cond8 = jnp.concatenate([c_ctx[None, :], c, jnp.zeros((8 - 1 - nb_lat, D_MODEL), F32)], axis=0)
    mods = _mods(cond8, w_ada, b_ada)
    mods = mods.reshape(DEPTH, 8, N_MOD, D_MODEL).transpose(0, 2, 1, 3)
    mods = mods[:, :, :1 + nb_lat, None, :]

    cos, sin = _rope_tables(lat)
    inv = jnp.stack([_inv_count_table(lat, False), _inv_count_table(lat, True)])
    pw = pool_w.astype(BF16)
    fw = final_norm_w.reshape(1, D_MODEL)

    def mlp_casts(layer):
        return (_Cast(mlp_w1, layer, 16), _Cast(mlp_w2, layer, 16))

    rsb = _rms(tk, x_prompt.reshape(tk.nc, D_MODEL), x_sample.reshape(nb_lat * lat, D_MODEL))
    x = None
    qkvgs = []
    mlp_w = {}
    for i in range(DEPTH):
        j = i // 2
        sh_a, sc_a, g_a, sh_m, sc_m, g_m = (mods[i, k] for k in range(N_MOD))
        nw_a = norm_mix_w[i].reshape(1, D_MODEL)
        if i % 2 == 0:
            casts = () if i in mlp_w else (_Cast(mlp_w1, i, 8), _Cast(mlp_w2, i, 8))
            if x is None:
                xc3, xl3, xl_off = x_prompt.reshape(1, tk.nc, D_MODEL), x_sample, 1
            else:
                xc3 = xl3 = x.reshape(1 + nb_lat, lat, D_MODEL)
                xl_off = 0
            x, *conv = _pool(xc3, xl3, xl_off, rsb.reshape(1 + nb_lat, lat, LANES),
                             inv, nw_a, sc_a, sh_a, g_a, pool_scale[j].reshape(1, D_MODEL),
                             pool_b[j].reshape(1, D_MODEL), pw, j, casts=casts)
            if conv:
                mlp_w[i] = conv
            x = x.reshape(tk.t, D_MODEL)
            host_casts = (_Cast(ret_w_in, j, 16, col_groups=RET_IN // (2 * RET_HK)),)
        else:
            later = [l for l in (i, i + 1) if l < DEPTH]
            qkvg, *conv = _inproj(tk, x, nw_a, sc_a, sh_a, w_in, cos, sin,
                                  casts=sum((mlp_casts(l) for l in later), ()))
            for n, l in enumerate(later):
                mlp_w[l] = conv[2 * n:2 * n + 2]
            qkvgs.append(qkvg)
            states_of = None
            if len(qkvgs) == 2:
                states_of = (qkvgs[0].reshape(tk.t // seq, seq, RET_IN), ret_decay)
            go_c, *new_state = _retcore(qkvg.reshape(tk.t // seq, seq, RET_IN), 0, nb_ctx, ret_decay[j],
                                        ret_gn_w[j], None, j, RET_HEADS, states_of=states_of)
            go_l, w_out = _retcore(qkvg.reshape(tk.t // lat, lat, RET_IN), tk.nc // lat, nb_lat,
                                   ret_decay[j], ret_gn_w[j], state_ret, j, 1,
                                   casts=(_Cast(ret_w_out, j, 8),))
            x = _outproj(tk, x, go_c.reshape(tk.nc, RET_HV), go_l.reshape(nb_lat * lat, RET_HV),
                         w_out, g_a)
            host_casts = ()
        w1, w2 = mlp_w[i]
        x, rsb, *conv = _mlp(tk_mlp, x, norm_mlp_w[i].reshape(1, D_MODEL), sc_m, sh_m, g_m, w1, w2, fw,
                             2048, i == DEPTH - 1, casts=host_casts)
        if conv:
            w_in, = conv
    y_prompt, y_sample = x, rsb
    return (y_prompt.reshape(x_prompt.shape), y_sample.reshape(x_sample.shape), new_state[0])
```
